```python
import jax, jax.numpy as jnp
from jax import lax
import numpy as np

D_MODEL = 1024
BATCH = 1
SEQ = 16384
DEPTH = 2
DEC_BATCH = 32
DEC_SEQ = 1
PAST_LEN = 16384
PAGE_SIZE = 128

W_BR = D_MODEL // 2
N_BRANCH = 3
D_CONV = W_BR
K_CONV = 31
H_F = W_BR // 64
DH_F = 64
Q_BLOCK = 128
H_M = 4
DV_M = W_BR // H_M
DK_M = DV_M // 2
M_CHUNK = 64
I_CAP = 15.0
N_MEM = 256
H_X = 4
DH_X = D_MODEL // H_X
N_EXPERTS = 32
TOP_K = 4
D_EXPERT = D_MODEL
SWIGLU_ALPHA = 1.702
SWIGLU_LIMIT = 7.0
MOE_BLOCK = 128
MOE_BLOCK_SMALL = 8
LN_EPS = 1e-5
IN_SPLIT = (D_CONV, D_CONV, H_F * DH_F, H_F * DH_F, H_F * DH_F, H_F,
            H_M * DK_M, H_M * DK_M, H_M * DV_M, H_M, H_M, H_M * DV_M, N_BRANCH * D_MODEL)
N_IN = sum(IN_SPLIT)

kernel_name = 'hybrid_conv_fox_mlstm_moe_decode_step'


def _layer_norm(x, g, b):
    xf = x.astype(jnp.float32)
    mu = jnp.mean(xf, -1, keepdims=True)
    var = jnp.mean(jnp.square(xf - mu), -1, keepdims=True)
    return ((xf - mu) * lax.rsqrt(var + LN_EPS) * g + b).astype(x.dtype)


def _split_in(z):
    parts, start = [], 0
    for size in IN_SPLIT:
        parts.append(z[..., start:start + size])
        start += size
    return parts


def _mixer_inputs(x, w_in, b_in, b_fox_f, b_mlstm_f):
    b, t, _ = x.shape
    (ca, cg, fq, fk, fv, ff, mq, mk, mv, mi, mf, mo, gt) = _split_in(x @ w_in + b_in)
    u = ca * jax.nn.sigmoid(cg)
    fq = fq.reshape(b, t, H_F, DH_F)
    fk = fk.reshape(b, t, H_F, DH_F)
    fv = fv.reshape(b, t, H_F, DH_F)
    flf = jax.nn.log_sigmoid(ff.astype(jnp.float32) + b_fox_f)
    mq = mq.reshape(b, t, H_M, DK_M)
    mk = mk.reshape(b, t, H_M, DK_M) * DK_M ** -0.5
    mv = mv.reshape(b, t, H_M, DV_M)
    mig = I_CAP * jnp.tanh(mi.astype(jnp.float32) / I_CAP)
    mlf = jax.nn.log_sigmoid(mf.astype(jnp.float32) + b_mlstm_f)
    mo = jax.nn.sigmoid(mo)
    gates = jax.nn.sigmoid(gt).reshape(b, t, N_BRANCH, D_MODEL)
    return (u, fq, fk, fv, flf, mq, mk, mv, mig, mlf, mo, gates)


def _conv_branch(u, buf, w_dw, b_dw, g, b):
    full = jnp.concatenate([buf.astype(u.dtype), u], axis=1)
    y = lax.conv_general_dilated(full, w_dw[:, None, :].astype(u.dtype), (1,), 'VALID',
                                 dimension_numbers=('NWC', 'WIO', 'NWC'),
                                 feature_group_count=D_CONV) + b_dw
    y = jax.nn.silu(_layer_norm(y, g, b))
    return y, full[:, -(K_CONV - 1):]


def _fox_prompt(q, k, v, lf):
    b, s, h, dh = q.shape
    nq = s // Q_BLOCK
    c = jnp.swapaxes(jnp.cumsum(lf, axis=1), 1, 2)
    qb = jnp.moveaxis(q.reshape(b, nq, Q_BLOCK, h, dh), 1, 0)
    cb = jnp.moveaxis(c.reshape(b, h, nq, Q_BLOCK), 2, 0)
    k_pos = jnp.arange(s)

    def block(args):
        qi, ci, i = args
        logits = jnp.einsum('bqhd,bkhd->bhqk', qi, k).astype(jnp.float32) * DH_F ** -0.5
        logits = logits + ci[..., :, None] - c[:, :, None, :]
        q_pos = i * Q_BLOCK + jnp.arange(Q_BLOCK)
        logits = jnp.where(k_pos[None, :] <= q_pos[:, None], logits, -jnp.inf)
        p = jax.nn.softmax(logits, axis=-1).astype(v.dtype)
        return jnp.einsum('bhqk,bkhd->bqhd', p, v)

    o = lax.map(block, (qb, cb, jnp.arange(nq)))
    return jnp.moveaxis(o, 0, 1).reshape(b, s, h, dh)


def _fox_sample(q, k, v, lf, k_past, v_past, lf_past):
    t = q.shape[1]
    p_len = k_past.shape[1]
    lf_past = lf_past.astype(jnp.float32)
    cn = jnp.swapaxes(jnp.cumsum(lf, axis=1), 1, 2)
    suffix = jnp.swapaxes(lax.cumsum(lf_past, axis=1, reverse=True) - lf_past, 1, 2)
    s_past = (jnp.einsum('bqhd,bkhd->bhqk', q, k_past).astype(jnp.float32) * DH_F ** -0.5
              + suffix[:, :, None, :] + cn[..., :, None])
    s_new = (jnp.einsum('bqhd,bkhd->bhqk', q, k).astype(jnp.float32) * DH_F ** -0.5
             + cn[..., :, None] - cn[..., None, :])
    s_new = jnp.where(jnp.tril(jnp.ones((t, t), bool)), s_new, -jnp.inf)
    p = jax.nn.softmax(jnp.concatenate([s_past, s_new], axis=-1), axis=-1).astype(v.dtype)
    return (jnp.einsum('bhqk,bkhd->bqhd', p[..., :p_len], v_past)
            + jnp.einsum('bhqk,bkhd->bqhd', p[..., p_len:], v))


def _mlstm_chunk(carry, inp):
    c0, n0, m0 = carry
    q, k, v, ig, lf = inp
    ln = q.shape[2]
    bc = jnp.cumsum(lf, axis=-1)
    d = bc[..., :, None] - bc[..., None, :] + ig[..., None, :]
    d = jnp.where(jnp.tril(jnp.ones((ln, ln), bool)), d, -jnp.inf)
    inter = bc + m0[..., None]
    m_row = jnp.maximum(jnp.max(d, axis=-1), inter)
    s_in = jnp.exp(inter - m_row)
    qk = jnp.einsum('bhtk,bhsk->bhts', q, k) * jnp.exp(d - m_row[..., None])
    num = (jnp.einsum('bhts,bhsv->bhtv', qk, v)
           + s_in[..., None] * jnp.einsum('bhtk,bhkv->bhtv', q, c0))
    den = jnp.sum(qk, axis=-1) + s_in * jnp.einsum('bhtk,bhk->bht', q, n0)
    h = num / jnp.maximum(jnp.abs(den), jnp.exp(-m_row))[..., None]
    b_last = bc[..., -1]
    dec = b_last[..., None] - bc + ig
    m_new = jnp.maximum(b_last + m0, jnp.max(dec, axis=-1))
    ws = jnp.exp(dec - m_new[..., None])
    s0 = jnp.exp(b_last + m0 - m_new)
    c_new = s0[..., None, None] * c0 + jnp.einsum('bhs,bhsk,bhsv->bhkv', ws, k, v)
    n_new = s0[..., None] * n0 + jnp.einsum('bhs,bhsk->bhk', ws, k)
    return (c_new, n_new, m_new), h


def _mlstm_prompt(q, k, v, ig, lf):
    b, s, h, dk = q.shape
    dv = v.shape[-1]
    nc = s // M_CHUNK

    def chunks(a):
        a = jnp.moveaxis(a.astype(jnp.float32), 2, 1)
        a = a.reshape(a.shape[:2] + (nc, M_CHUNK) + a.shape[3:])
        return jnp.moveaxis(a, 2, 0)

    init = (jnp.zeros((b, h, dk, dv), jnp.float32), jnp.zeros((b, h, dk), jnp.float32),
            jnp.zeros((b, h), jnp.float32))
    state, hs = lax.scan(_mlstm_chunk, init, (chunks(q), chunks(k), chunks(v), chunks(ig), chunks(lf)))
    hs = jnp.moveaxis(hs, 0, 2).reshape(b, h, s, dv)
    return jnp.swapaxes(hs, 1, 2), state


def _mlstm_sample(q, k, v, ig, lf, c0, n0, m0):
    def to_bh(a):
        return jnp.swapaxes(a.astype(jnp.float32), 1, 2)
    state, h = _mlstm_chunk((c0.astype(jnp.float32), n0.astype(jnp.float32), m0.astype(jnp.float32)),
                            (to_bh(q), to_bh(k), to_bh(v), to_bh(ig), to_bh(lf)))
    return jnp.swapaxes(h, 1, 2), state


def _merge(conv_y, fox_y, m_h, m_o, gates, g_mnorm, w_branch, w_out):
    b, t = conv_y.shape[:2]
    hf = m_h.astype(jnp.float32)
    hn = (hf * lax.rsqrt(jnp.mean(hf * hf, -1, keepdims=True) + LN_EPS)).reshape(b, t, H_M * DV_M)
    mlstm_y = (hn * g_mnorm * m_o).astype(conv_y.dtype)
    br = jnp.stack([conv_y, fox_y.reshape(b, t, H_F * DH_F).astype(conv_y.dtype), mlstm_y], axis=2)
    proj = jnp.einsum('btgc,gcd->btgd', br, w_branch)
    return jnp.sum(gates * proj, axis=2) @ w_out


def _mem_attn(x, mem_k, mem_v, w_xq, w_xo):
    b, t, _ = x.shape
    q = (x @ w_xq).reshape(b, t, H_X, DH_X)
    s = jnp.einsum('bthd,bmhd->bhtm', q, mem_k).astype(jnp.float32) * DH_X ** -0.5
    p = jax.nn.softmax(s, axis=-1).astype(mem_v.dtype)
    o = jnp.einsum('bhtm,bmhd->bthd', p, mem_v).reshape(b, t, H_X * DH_X)
    return o @ w_xo


def _moe(x, w_router, b_router, w_gu, b_gu, w_down, b_down):
    n, d = x.shape
    logits = (x @ w_router + b_router).astype(jnp.float32)
    top_v, top_e = lax.top_k(logits, TOP_K)
    gate = jax.nn.softmax(top_v, axis=-1)
    a = n * TOP_K
    blk = MOE_BLOCK if a >= MOE_BLOCK * N_EXPERTS else MOE_BLOCK_SMALL
    n_blk = -(-a // blk) + N_EXPERTS
    rows = n_blk * blk
    e_flat = top_e.reshape(a)
    tok_flat = jnp.repeat(jnp.arange(n, dtype=jnp.int32), TOP_K)
    g_flat = gate.reshape(a)
    order = jnp.argsort(e_flat)
    e_sorted = e_flat[order]
    counts = jnp.bincount(e_flat, length=N_EXPERTS)
    start = jnp.cumsum(counts) - counts
    padded = (counts + blk - 1) // blk * blk
    pend = jnp.cumsum(padded)
    dest = (pend - padded)[e_sorted] + jnp.arange(a) - start[e_sorted]
    row_tok = jnp.full((rows,), n, jnp.int32).at[dest].set(tok_flat[order])
    row_gate = jnp.zeros((rows,), jnp.float32).at[dest].set(g_flat[order])
    blk_expert = jnp.minimum(jnp.searchsorted(pend, jnp.arange(n_blk) * blk, side='right'), N_EXPERTS - 1)
    x_rows = jnp.concatenate([x, jnp.zeros((1, d), x.dtype)], 0)[row_tok].reshape(n_blk, blk, d)

    def expert_block(args):
        xe, e = args
        gu = xe @ w_gu[e] + b_gu[e]
        g = jnp.minimum(gu[:, :D_EXPERT], SWIGLU_LIMIT)
        u = jnp.clip(gu[:, D_EXPERT:], -SWIGLU_LIMIT, SWIGLU_LIMIT)
        return ((u + 1.0) * g * jax.nn.sigmoid(SWIGLU_ALPHA * g)) @ w_down[e] + b_down[e]

    y_rows = lax.map(expert_block, (x_rows, blk_expert)).reshape(rows, d)
    y = jnp.zeros((n + 1, d), y_rows.dtype).at[row_tok].add(y_rows * row_gate[:, None].astype(y_rows.dtype))
    return y[:n].astype(x.dtype)


def setup_inputs(seed: int = 0) -> dict:
    key = jax.random.key(seed)
    keys = iter(jax.random.split(key, 64))

    def nrm(shape, scale=1.0):
        return jax.random.normal(next(keys), shape, jnp.float32) * scale

    def gain(shape):
        return 1.0 + nrm(shape, 0.02)

    n_pages = PAST_LEN // PAGE_SIZE
    n_used = DEC_BATCH * n_pages
    n_pool = n_used + n_used // 4
    page_table = jax.random.permutation(next(keys), n_pool)[:n_used].reshape(DEC_BATCH, n_pages).astype(jnp.int32)
    beta = (8.0 * DEPTH) ** -0.25
    fan = D_MODEL ** -0.5
    return {
        'x_prompt': nrm((BATCH, SEQ, D_MODEL)),
        'x_sample': nrm((DEC_BATCH, DEC_SEQ, D_MODEL)),
        'mem_prompt': nrm((BATCH, N_MEM, D_MODEL)),
        'cache_fox_k': nrm((DEPTH, n_pool, PAGE_SIZE, H_F, DH_F)),
        'cache_fox_v': nrm((DEPTH, n_pool, PAGE_SIZE, H_F, DH_F)),
        'cache_fox_lf': jax.nn.log_sigmoid(nrm((DEPTH, n_pool, PAGE_SIZE, H_F)) + 6.0),
        'page_table': page_table,
        'state_conv': nrm((DEPTH, DEC_BATCH, K_CONV - 1, D_CONV), 0.5),
        'state_mlstm_c': nrm((DEPTH, DEC_BATCH, H_M, DK_M, DV_M), 0.1),
        'state_mlstm_n': nrm((DEPTH, DEC_BATCH, H_M, DK_M), 0.1),
        'state_mlstm_m': nrm((DEPTH, DEC_BATCH, H_M), 0.5),
        'cache_mem_k': nrm((DEPTH, DEC_BATCH, N_MEM, H_X, DH_X)),
        'cache_mem_v': nrm((DEPTH, DEC_BATCH, N_MEM, H_X, DH_X)),
        'w_in': nrm((DEPTH, D_MODEL, N_IN), fan),
        'b_in': nrm((DEPTH, N_IN), 0.02),
        'b_fox_f': jnp.linspace(2.0, 8.0, H_F)[None, :] + nrm((DEPTH, H_F), 0.1),
        'b_mlstm_f': jnp.linspace(3.0, 6.0, H_M)[None, :] + nrm((DEPTH, H_M), 0.1),
        'w_dw': nrm((DEPTH, K_CONV, D_CONV), K_CONV ** -0.5),
        'b_dw': nrm((DEPTH, D_CONV), 0.02),
        'g_conv_ln': gain((DEPTH, D_CONV)),
        'b_conv_ln': nrm((DEPTH, D_CONV), 0.02),
        'g_mlstm_norm': gain((DEPTH, H_M * DV_M)),
        'w_branch': nrm((DEPTH, N_BRANCH, W_BR, D_MODEL), W_BR ** -0.5 * beta),
        'w_mix_out': nrm((DEPTH, D_MODEL, D_MODEL), fan * beta),
        'g_ln1': gain((DEPTH, D_MODEL)),
        'b_ln1': nrm((DEPTH, D_MODEL), 0.02),
        'w_xq': nrm((DEPTH, D_MODEL, H_X * DH_X), fan),
        'w_xk': nrm((DEPTH, D_MODEL, H_X * DH_X), fan),
        'w_xv': nrm((DEPTH, D_MODEL, H_X * DH_X), fan * beta),
        'w_xo': nrm((DEPTH, H_X * DH_X, D_MODEL), fan * beta),
        'g_ln2': gain((DEPTH, D_MODEL)),
        'b_ln2': nrm((DEPTH, D_MODEL), 0.02),
        'w_router': nrm((DEPTH, D_MODEL, N_EXPERTS), fan),
        'b_router': nrm((DEPTH, N_EXPERTS), 0.01),
        'w_gate_up': nrm((DEPTH, N_EXPERTS, D_MODEL, 2 * D_EXPERT), fan),
        'b_gate_up': nrm((DEPTH, N_EXPERTS, 2 * D_EXPERT), 0.02),
        'w_down': nrm((DEPTH, N_EXPERTS, D_EXPERT, D_MODEL), D_EXPERT ** -0.5 * beta),
        'b_down': nrm((DEPTH, N_EXPERTS, D_MODEL), 0.02),
        'g_ln3': gain((DEPTH, D_MODEL)),
        'b_ln3': nrm((DEPTH, D_MODEL), 0.02),
    }


def reference(x_prompt, x_sample, mem_prompt, cache_fox_k, cache_fox_v, cache_fox_lf, page_table,
              state_conv, state_mlstm_c, state_mlstm_n, state_mlstm_m, cache_mem_k, cache_mem_v,
              w_in, b_in, b_fox_f, b_mlstm_f, w_dw, b_dw, g_conv_ln, b_conv_ln, g_mlstm_norm,
              w_branch, w_mix_out, g_ln1, b_ln1, w_xq, w_xk, w_xv, w_xo, g_ln2, b_ln2,
              w_router, b_router, w_gate_up, b_gate_up, w_down, b_down, g_ln3, b_ln3):
    alpha = (2.0 * DEPTH) ** 0.25
    b_p = x_prompt.shape[0]
    b_s = x_sample.shape[0]
    xp, xs = x_prompt, x_sample
    rows_p, rows_s = [], []
    for l in range(DEPTH):
        (u, fq, fk, fv, flf, mq, mk, mv, mig, mlf, mo, gates) = _mixer_inputs(xp, w_in[l], b_in[l], b_fox_f[l], b_mlstm_f[l])
        conv_y, conv_st = _conv_branch(u, jnp.zeros((b_p, K_CONV - 1, D_CONV), u.dtype),
                                       w_dw[l], b_dw[l], g_conv_ln[l], b_conv_ln[l])
        fox_y = _fox_prompt(fq, fk, fv, flf)
        m_h, (m_c, m_n, m_m) = _mlstm_prompt(mq, mk, mv, mig, mlf)
        mix = _merge(conv_y, fox_y, m_h, mo, gates, g_mlstm_norm[l], w_branch[l], w_mix_out[l])
        xp = _layer_norm(alpha * xp + mix, g_ln1[l], b_ln1[l])
        mem_k = (mem_prompt @ w_xk[l]).reshape(b_p, -1, H_X, DH_X)
        mem_v = (mem_prompt @ w_xv[l]).reshape(b_p, -1, H_X, DH_X)
        xp = _layer_norm(alpha * xp + _mem_attn(xp, mem_k, mem_v, w_xq[l], w_xo[l]), g_ln2[l], b_ln2[l])
        ff = _moe(xp.reshape(-1, D_MODEL), w_router[l], b_router[l], w_gate_up[l], b_gate_up[l], w_down[l], b_down[l])
        xp = _layer_norm(alpha * xp + ff.reshape(xp.shape), g_ln3[l], b_ln3[l])
        rows_p.append((fk, fv, flf, conv_st, m_c, m_n, m_m, mem_k, mem_v))

        (u, fq, fk, fv, flf, mq, mk, mv, mig, mlf, mo, gates) = _mixer_inputs(xs, w_in[l], b_in[l], b_fox_f[l], b_mlstm_f[l])
        conv_y, conv_st = _conv_branch(u, state_conv[l], w_dw[l], b_dw[l], g_conv_ln[l], b_conv_ln[l])
        k_past = cache_fox_k[l][page_table].reshape(b_s, -1, H_F, DH_F)
        v_past = cache_fox_v[l][page_table].reshape(b_s, -1, H_F, DH_F)
        lf_past = cache_fox_lf[l][page_table].reshape(b_s, -1, H_F)
        fox_y = _fox_sample(fq, fk, fv, flf, k_past, v_past, lf_past)
        m_h, (m_c, m_n, m_m) = _mlstm_sample(mq, mk, mv, mig, mlf, state_mlstm_c[l], state_mlstm_n[l], state_mlstm_m[l])
        mix = _merge(conv_y, fox_y, m_h, mo, gates, g_mlstm_norm[l], w_branch[l], w_mix_out[l])
        xs = _layer_norm(alpha * xs + mix, g_ln1[l], b_ln1[l])
        xs = _layer_norm(alpha * xs + _mem_attn(xs, cache_mem_k[l], cache_mem_v[l], w_xq[l], w_xo[l]), g_ln2[l], b_ln2[l])
        ff = _moe(xs.reshape(-1, D_MODEL), w_router[l], b_router[l], w_gate_up[l], b_gate_up[l], w_down[l], b_down[l])
        xs = _layer_norm(alpha * xs + ff.reshape(xs.shape), g_ln3[l], b_ln3[l])
        rows_s.append((fk, fv, flf, conv_st, m_c, m_n, m_m))

    (fox_k_p, fox_v_p, fox_lf_p, conv_p, mlstm_c_p, mlstm_n_p, mlstm_m_p, mem_k_p, mem_v_p) = [jnp.stack(a) for a in zip(*rows_p)]
    (fox_k_s, fox_v_s, fox_lf_s, conv_s, mlstm_c_s, mlstm_n_s, mlstm_m_s) = [jnp.stack(a) for a in zip(*rows_s)]
    return (xp, xs, fox_k_p, fox_v_p, fox_lf_p, conv_p, mlstm_c_p, mlstm_n_p, mlstm_m_p, mem_k_p, mem_v_p,
            fox_k_s, fox_v_s, fox_lf_s, conv_s, mlstm_c_s, mlstm_n_s, mlstm_m_s)
```

```python
import functools

import numpy as np
import jax
import jax.numpy as jnp
from jax import lax
from jax.experimental import pallas as pl
from jax.experimental.pallas import tpu as pltpu

F32 = jnp.float32
BF16 = jnp.bfloat16
I32 = jnp.int32

D_MODEL = 1024
W_BR = 512
K_CONV = 31
H_F = 8
DH_F = 64
H_M = 4
DK_M = 64
DV_M = 128
M_CHUNK = 64
I_CAP = 15.0
H_X = 4
DH_X = 256
N_EXPERTS = 32
TOP_K = 4
D_EXPERT = 1024
SWIGLU_ALPHA = 1.702
SWIGLU_LIMIT = 7.0
LN_EPS = 1e-5
DEPTH = 2
ALPHA = (2.0 * DEPTH) ** 0.25
NEG_BIG = -1e30
LANES = 128
VMEM_LIMIT = 56 * 1024 * 1024


def _cparams(*sem):
    return pltpu.CompilerParams(dimension_semantics=sem, vmem_limit_bytes=VMEM_LIMIT)


def _dot(a, b):
    return jnp.dot(a, b, preferred_element_type=F32)


def _dot_nt(a, b):
    return lax.dot_general(a, b, (((1,), (1,)), ((), ())), preferred_element_type=F32)


def _dot_tn(a, b):
    return lax.dot_general(a, b, (((0,), (0,)), ((), ())), preferred_element_type=F32)


def _split3(a):
    a1 = a.astype(BF16)
    r = a - a1.astype(F32)
    a2 = r.astype(BF16)
    r = r - a2.astype(F32)
    return a1, a2, r.astype(BF16)


def _log_sigmoid(x):
    return jnp.minimum(x, 0.0) - jnp.log1p(jnp.exp(-jnp.abs(x)))


def _layer_norm(x, g, b):
    mu = jnp.mean(x, axis=-1, keepdims=True)
    xc = x - mu
    var = jnp.mean(xc * xc, axis=-1, keepdims=True)
    return xc * lax.rsqrt(var + LN_EPS) * g + b


def _row_block(n, pref):
    return pref if n % pref == 0 else n


def _gate_act(z, idx):
    ls = _log_sigmoid(z)
    tg = I_CAP * jnp.tanh(z * (1.0 / I_CAP))
    return jnp.where(idx < 8, ls, jnp.where(idx < 12, tg, jnp.where(idx < 16, ls, 0.0)))


def _inproj_kernel(x_ref, wc_ref, bc_ref, wf_ref, bf_ref, wm_ref, bm_ref, ws_ref, bs_ref, wst_ref, bst_ref,
                   ufull_ref, ublk_ref, lblk_ref,
                   u_ref, fqb_ref, fk_ref, fv_ref, fkb_ref, fvb_ref, mq_ref, mk_ref, mv_ref, mo_ref,
                   cols_ref, ccols_ref, rows_ref, carry_ref):
    i = pl.program_id(0)

    @pl.when(i == 0)
    def _():
        carry_ref[...] = jnp.zeros_like(carry_ref)

    tm = x_ref.shape[0]
    xb = x_ref[...].astype(BF16)
    zc = _dot(xb, wc_ref[...]) + bc_ref[...]
    u_ref[...] = zc[:, :W_BR] * jax.nn.sigmoid(zc[:, W_BR:])

    zf = _dot(xb, wf_ref[...]) + bf_ref[...]
    fk = zf[:, W_BR:2 * W_BR]
    fv = zf[:, 2 * W_BR:]
    fqb_ref[...] = (zf[:, :W_BR] * DH_F ** -0.5).astype(BF16)
    fk_ref[...] = fk
    fv_ref[...] = fv
    fkb_ref[...] = fk.astype(BF16)
    fvb_ref[...] = fv.astype(BF16)

    zm = _dot(xb, wm_ref[...]) + bm_ref[...]
    mq_ref[...] = zm[:, :256]
    mk_ref[...] = zm[:, 256:512] * DK_M ** -0.5
    mv_ref[...] = zm[:, 512:1024]
    mo_ref[...] = jax.nn.sigmoid(zm[:, 1024:])

    zs = _dot(xb, ws_ref[...]) + bs_ref[...]
    lane = lax.broadcasted_iota(I32, zs.shape, 1)
    act_c = _gate_act(zs, lane)
    cols_ref[...] = act_c
    c1, c2, c3 = _split3(act_c)
    lb = lblk_ref[...]
    ccols_ref[...] = _dot(lb, c1) + _dot(lb, c2) + _dot(lb, c3)

    zt = _dot_nt(wst_ref[...], xb) + bst_ref[...]
    row = lax.broadcasted_iota(I32, zt.shape, 0)
    act_r = _gate_act(zt, row)
    r1, r2, r3 = _split3(act_r)
    uf = ufull_ref[...]
    ub = ublk_ref[...]
    cum_full = _dot(r1, uf) + _dot(r2, uf) + _dot(r3, uf) + carry_ref[:, 0:1]
    cum_blk = _dot(r1, ub) + _dot(r2, ub) + _dot(r3, ub)
    carry_ref[...] = jnp.broadcast_to(cum_full[:, tm - 1:tm], carry_ref.shape)
    rows_ref[0:16, :] = act_r
    rows_ref[16:32, :] = jnp.where(row < 8, cum_full, cum_blk)


def _tri_consts(tm, chunk):
    t = np.arange(tm)
    upper = (t[:, None] <= t[None, :])
    same = (t[:, None] // chunk) == (t[None, :] // chunk)
    ufull = jnp.asarray(upper, BF16)
    ublk = jnp.asarray(upper & same, BF16)
    lblk = jnp.asarray(upper.T & same, BF16)
    return ufull, ublk, lblk


def _split_w_in(w_in, b_in, b_fox_f, b_mlstm_f):
    o = np.cumsum([0, 512, 512, 512, 512, 512, 8, 256, 256, 512, 4, 4, 512, 3072])
    sl = lambda a, i, j: a[..., o[i]:o[j]]
    wc, bc = sl(w_in, 0, 2), sl(b_in, 0, 2)
    wf, bf = sl(w_in, 2, 5), sl(b_in, 2, 5)
    wm = jnp.concatenate([sl(w_in, 6, 9), sl(w_in, 11, 12)], axis=-1)
    bm = jnp.concatenate([sl(b_in, 6, 9), sl(b_in, 11, 12)], axis=-1)
    wsm = jnp.concatenate([sl(w_in, 5, 6), sl(w_in, 9, 11)], axis=-1)
    bsm = jnp.concatenate([sl(b_in, 5, 6) + b_fox_f, sl(b_in, 9, 10), sl(b_in, 10, 11) + b_mlstm_f], axis=-1)
    ws = jnp.pad(wsm, ((0, 0), (0, LANES - 16)))
    bs = jnp.pad(bsm, (0, LANES - 16))
    wg, bg = sl(w_in, 12, 13), sl(b_in, 12, 13)
    return dict(wc=wc.astype(BF16), bc=bc[None], wf=wf.astype(BF16), bf=bf[None], wm=wm.astype(BF16), bm=bm[None],
                ws=ws.astype(BF16), bs=bs[None], wst=wsm.T.astype(BF16), bst=bsm[:, None],
                wg=wg.astype(BF16), bg=bg[None])


def _inproj(x, w, tm):
    r = x.shape[0]
    chunk = min(M_CHUNK, tm)
    ufull, ublk, lblk = _tri_consts(tm, chunk)
    rowb = lambda n: pl.BlockSpec((tm, n), lambda i: (i, 0))
    full = lambda a: pl.BlockSpec(a.shape, lambda i: (0,) * a.ndim)
    consts = [w['wc'], w['bc'], w['wf'], w['bf'], w['wm'], w['bm'], w['ws'], w['bs'], w['wst'], w['bst'],
              ufull, ublk, lblk]
    out_shape = [jax.ShapeDtypeStruct((r, 512), F32), jax.ShapeDtypeStruct((r, 512), BF16),
                 jax.ShapeDtypeStruct((r, 512), F32), jax.ShapeDtypeStruct((r, 512), F32),
                 jax.ShapeDtypeStruct((r, 512), BF16), jax.ShapeDtypeStruct((r, 512), BF16),
                 jax.ShapeDtypeStruct((r, 256), F32), jax.ShapeDtypeStruct((r, 256), F32),
                 jax.ShapeDtypeStruct((r, 512), F32), jax.ShapeDtypeStruct((r, 512), F32),
                 jax.ShapeDtypeStruct((r, LANES), F32), jax.ShapeDtypeStruct((r, LANES), F32),
                 jax.ShapeDtypeStruct((32, r), F32)]
    out_specs = [rowb(512)] * 6 + [rowb(256)] * 2 + [rowb(512)] * 2 + [rowb(LANES)] * 2 + \
                [pl.BlockSpec((32, tm), lambda i: (0, i))]
    outs = pl.pallas_call(
        _inproj_kernel, grid=(r // tm,),
        in_specs=[rowb(D_MODEL)] + [full(a) for a in consts],
        out_specs=out_specs, out_shape=out_shape,
        scratch_shapes=[pltpu.VMEM((16, LANES), F32)],
        compiler_params=_cparams("arbitrary"), name="inproj",
    )(x, *consts)
    keys = ['u', 'fqb', 'fk', 'fv', 'fkb', 'fvb', 'mq', 'mk', 'mv', 'mo', 'cols', 'ccols', 'rows']
    return dict(zip(keys, outs))


HALO = 32


def _conv_kernel(u_ref, w_ref, b_ref, g_ref, bln_ref, y_ref, buf_ref):
    i = pl.program_id(0)
    tm = u_ref.shape[0]

    @pl.when(i == 0)
    def _():
        buf_ref[0:HALO, :] = jnp.zeros((HALO, W_BR), F32)

    buf_ref[HALO:HALO + tm, :] = u_ref[...]
    acc = jnp.zeros((tm, W_BR), F32) + b_ref[...]
    off = HALO - (K_CONV - 1)
    for k in range(K_CONV):
        acc = acc + w_ref[k:k + 1, :] * buf_ref[off + k:off + k + tm, :]
    yn = _layer_norm(acc, g_ref[...], bln_ref[...])
    y_ref[...] = (yn * jax.nn.sigmoid(yn)).astype(y_ref.dtype)
    buf_ref[0:HALO, :] = buf_ref[tm:tm + HALO, :]


def _conv_prompt(u, w_dw, b_dw, g, b, tm):
    s = u.shape[0]
    full = lambda a: pl.BlockSpec(a.shape, lambda i: (0,) * a.ndim)
    args = [w_dw, b_dw[None], g[None], b[None]]
    return pl.pallas_call(
        _conv_kernel, grid=(s // tm,),
        in_specs=[pl.BlockSpec((tm, W_BR), lambda i: (i, 0))] + [full(a) for a in args],
        out_specs=pl.BlockSpec((tm, W_BR), lambda i: (i, 0)),
        out_shape=jax.ShapeDtypeStruct((s, W_BR), BF16),
        scratch_shapes=[pltpu.VMEM((HALO + tm, W_BR), F32)],
        compiler_params=_cparams("arbitrary"), name="conv_prompt",
    )(u, *args)


def _conv_step_kernel(st_ref, u_ref, w_ref, b_ref, g_ref, bln_ref, y_ref, nst_ref):
    st = st_ref[0]
    un = u_ref[0]
    acc = b_ref[...] + w_ref[K_CONV - 1:K_CONV, :] * un
    acc = acc + jnp.sum(w_ref[0:K_CONV - 1, :] * st, axis=0, keepdims=True)
    yn = _layer_norm(acc, g_ref[...], bln_ref[...])
    y_ref[0] = (yn * jax.nn.sigmoid(yn)).astype(y_ref.dtype)
    nst_ref[0, 0:K_CONV - 2, :] = st[1:, :]
    nst_ref[0, K_CONV - 2:K_CONV - 1, :] = un


def _conv_step(state, u, w_dw, b_dw, g, b):
    bsz = u.shape[0]
    full = lambda a: pl.BlockSpec(a.shape, lambda i: (0,) * a.ndim)
    args = [w_dw, b_dw[None], g[None], b[None]]
    y, nst = pl.pallas_call(
        _conv_step_kernel, grid=(bsz,),
        in_specs=[pl.BlockSpec((1, K_CONV - 1, W_BR), lambda i: (i, 0, 0)),
                  pl.BlockSpec((1, 1, W_BR), lambda i: (i, 0, 0))] + [full(a) for a in args],
        out_specs=[pl.BlockSpec((1, 1, W_BR), lambda i: (i, 0, 0)),
                   pl.BlockSpec((1, K_CONV - 1, W_BR), lambda i: (i, 0, 0))],
        out_shape=[jax.ShapeDtypeStruct((bsz, 1, W_BR), BF16),
                   jax.ShapeDtypeStruct((bsz, K_CONV - 1, W_BR), F32)],
        compiler_params=_cparams("arbitrary"), name="conv_step",
    )(state, u[:, None, :], *args)
    return y[:, 0, :], nst


def _fox_prompt_kernel(q_ref, k_ref, v_ref, c_ref, o_ref, *, tq, tk):
    qi = pl.program_id(1)
    q2 = q_ref[...].astype(F32)
    lane = lax.broadcasted_iota(I32, (tq, LANES), 1)
    q_start = pl.multiple_of(qi * tq, tq)
    n_full = (qi * tq) // tk
    outs = []
    for hh in range(2):
        own = (lane < DH_F) if hh == 0 else (lane >= DH_F)
        qm = jnp.where(own, q2, 0.0).astype(BF16)
        c0 = c_ref[hh:hh + 1, pl.ds(q_start, LANES)][:, 0:1]

        def step(j, carry, masked):
            m, l, acc = carry
            k_start = pl.multiple_of(j * tk, tk)
            kb = k_ref[pl.ds(k_start, tk), :]
            vb = v_ref[pl.ds(k_start, tk), :]
            s = _dot_nt(qm, kb) + (c0 - c_ref[hh:hh + 1, pl.ds(k_start, tk)])
            if masked:
                rows = q_start + lax.broadcasted_iota(I32, (tq, tk), 0)
                cols = k_start + lax.broadcasted_iota(I32, (tq, tk), 1)
                s = jnp.where(cols <= rows, s, NEG_BIG)
            m_new = jnp.maximum(m, jnp.max(s, axis=1, keepdims=True))
            a = jnp.exp(m - m_new)
            p = jnp.exp(s - m_new)
            l = a * l + jnp.sum(p, axis=1, keepdims=True)
            acc = a * acc + _dot(p.astype(BF16), vb)
            return m_new, l, acc

        init = (jnp.full((tq, 1), NEG_BIG, F32), jnp.zeros((tq, 1), F32), jnp.zeros((tq, LANES), F32))
        carry = lax.fori_loop(0, n_full, functools.partial(step, masked=False), init)
        m, l, acc = step(n_full, carry, True)
        outs.append(acc / l)
    o_ref[...] = jnp.where(lane < DH_F, outs[0], outs[1]).astype(o_ref.dtype)


def _fox_prompt(fqb, fkb, fvb, crow, tq, tk):
    s = fqb.shape[0]
    c3 = crow.reshape(H_F // 2, 2, s)
    return pl.pallas_call(
        functools.partial(_fox_prompt_kernel, tq=tq, tk=tk), grid=(H_F // 2, s // tq),
        in_specs=[pl.BlockSpec((tq, LANES), lambda p, i: (i, p)),
                  pl.BlockSpec((s, LANES), lambda p, i: (0, p)),
                  pl.BlockSpec((s, LANES), lambda p, i: (0, p)),
                  pl.BlockSpec((None, 2, s), lambda p, i: (p, 0, 0))],
        out_specs=pl.BlockSpec((tq, LANES), lambda p, i: (i, p)),
        out_shape=jax.ShapeDtypeStruct((s, W_BR), BF16),
        compiler_params=_cparams("arbitrary", "arbitrary"), name="fox_prompt",
    )(fqb, fkb, fvb, c3)


def _fox_decode_kernel(pt_ref, q_ref, k_ref, v_ref, lf_ref, kn_ref, vn_ref, lfn_ref, tri_ref, o_ref,
                       m_ref, l_ref, acc_ref, suf_ref):
    p = pl.program_id(1)
    n_p = pl.num_programs(1)
    w = q_ref.shape[-1]
    row = lax.broadcasted_iota(I32, (H_F, w), 0)
    lane = lax.broadcasted_iota(I32, (H_F, w), 1)
    own = (lane // DH_F) == row
    qrows_f = jnp.where(own, jnp.broadcast_to(q_ref[...].astype(F32), (H_F, w)), 0.0)
    qrows = qrows_f.astype(BF16)

    @pl.when(p == 0)
    def _():
        m_ref[...] = jnp.full_like(m_ref, NEG_BIG)
        l_ref[...] = jnp.zeros_like(l_ref)
        acc_ref[...] = jnp.zeros_like(acc_ref)
        suf_ref[...] = jnp.zeros_like(suf_ref)

    s = _dot_nt(qrows, k_ref[...].astype(BF16))
    l1, l2, l3 = _split3(lf_ref[...])
    tri = tri_ref[0]
    ones = tri_ref[1]
    suf = _dot_tn(l1, tri) + _dot_tn(l2, tri) + _dot_tn(l3, tri)
    tot = _dot_tn(l1, ones) + _dot_tn(l2, ones) + _dot_tn(l3, ones)
    s = s + suf + suf_ref[:, 0:1] + lfn_ref[...]
    m_old = m_ref[:, 0:1]
    m_new = jnp.maximum(m_old, jnp.max(s, axis=1, keepdims=True))
    a = jnp.exp(m_old - m_new)
    pr = jnp.exp(s - m_new)
    l_new = a * l_ref[:, 0:1] + jnp.sum(pr, axis=1, keepdims=True)
    acc = a * acc_ref[...] + _dot(pr.astype(BF16), v_ref[...].astype(BF16))
    m_ref[...] = jnp.broadcast_to(m_new, m_ref.shape)
    l_ref[...] = jnp.broadcast_to(l_new, l_ref.shape)
    acc_ref[...] = acc
    suf_ref[...] = suf_ref[...] + tot[:, 0:1]

    @pl.when(p == n_p - 1)
    def _():
        kn = kn_ref[...].astype(BF16).astype(F32)
        s_new = jnp.sum(qrows_f * kn, axis=1, keepdims=True)
        m_fin = jnp.maximum(m_new, s_new)
        a2 = jnp.exp(m_new - m_fin)
        p_new = jnp.exp(s_new - m_fin)
        l_fin = a2 * l_new + p_new
        acc2 = a2 * acc + p_new * vn_ref[...]
        o_ref[...] = jnp.sum(jnp.where(own, acc2 / l_fin, 0.0), axis=0, keepdims=True)


def _fox_decode(page_table, fqb, fk_new, fv_new, lf_new_cols, cache_k, cache_v, cache_lf, layer):
    bsz, n_pages = page_table.shape
    n_pool, page = cache_k.shape[1], cache_k.shape[2]
    w = H_F * DH_F
    ck = cache_k.reshape(-1, page, w)
    cv = cache_v.reshape(-1, page, w)
    clf = cache_lf.reshape(-1, page, H_F)
    base = layer * n_pool
    t = np.arange(page)
    tri = jnp.asarray(np.stack([t[:, None] > t[None, :], np.ones((page, page), bool)]), BF16)
    pidx = lambda b, p, pt: (base + pt[b, n_pages - 1 - p], 0, 0)
    per_b = lambda b, p, pt: (b, 0, 0)
    grid_spec = pltpu.PrefetchScalarGridSpec(
        num_scalar_prefetch=1, grid=(bsz, n_pages),
        in_specs=[pl.BlockSpec((None, 1, w), per_b),
                  pl.BlockSpec((None, page, w), pidx),
                  pl.BlockSpec((None, page, w), pidx),
                  pl.BlockSpec((None, page, H_F), pidx),
                  pl.BlockSpec((None, 1, w), per_b),
                  pl.BlockSpec((None, 1, w), per_b),
                  pl.BlockSpec((None, H_F, 1), per_b),
                  pl.BlockSpec((2, page, page), lambda b, p, pt: (0, 0, 0))],
        out_specs=pl.BlockSpec((None, 1, w), per_b),
        scratch_shapes=[pltpu.VMEM((H_F, LANES), F32), pltpu.VMEM((H_F, LANES), F32),
                        pltpu.VMEM((H_F, w), F32), pltpu.VMEM((H_F, LANES), F32)])
    out = pl.pallas_call(
        _fox_decode_kernel, grid_spec=grid_spec,
        out_shape=jax.ShapeDtypeStruct((bsz, 1, w), F32),
        compiler_params=_cparams("arbitrary", "arbitrary"), name="fox_decode",
    )(page_table, fqb[:, None, :], ck, cv, clf, fk_new[:, None, :], fv_new[:, None, :], lf_new_cols[:, :, None], tri)
    return out[:, 0, :]


def _mlstm_prompt_kernel(q_ref, k_ref, v_ref, cols_ref, ccols_ref, rows_ref, h_ref, cst_ref, mst_ref,
                         c_scr, m_scr, *, n_chunks, ln):
    i = pl.program_id(0)

    @pl.when(i == 0)
    def _():
        c_scr[...] = jnp.zeros_like(c_scr)
        m_scr[...] = jnp.zeros_like(m_scr)

    lane = lax.broadcasted_iota(I32, (ln, LANES), 1)
    one_col = jnp.where(lane == 0, 1.0, 0.0).astype(F32)
    tril = lax.broadcasted_iota(I32, (ln, ln), 1) <= lax.broadcasted_iota(I32, (ln, ln), 0)
    srow = lax.broadcasted_iota(I32, (LANES, 2 * LANES), 0)

    def chunk(c, _):
        t0 = pl.multiple_of(c * ln, ln)
        cols = cols_ref[pl.ds(t0, ln), :]
        ccols = ccols_ref[pl.ds(t0, ln), :]
        rows = rows_ref[c]
        for p in range(H_M // 2):
            q2 = q_ref[pl.ds(t0, ln), p * LANES:(p + 1) * LANES]
            k2 = k_ref[pl.ds(t0, ln), p * LANES:(p + 1) * LANES].astype(BF16)
            c_pair = c_scr[p]
            c_pair_b = c_pair.astype(BF16)
            new_pair = c_pair
            for hh in range(2):
                h = 2 * p + hh
                own = (lane < DK_M) if hh == 0 else (lane >= DK_M)
                qm = jnp.where(own, q2, 0.0).astype(BF16)
                v = v_ref[pl.ds(t0, ln), h * DV_M:(h + 1) * DV_M]
                v_aug = jnp.concatenate([v, one_col], axis=1)
                ig_col = cols[:, 8 + h:9 + h]
                bc_col = ccols[:, 12 + h:13 + h]
                ig_row = rows[8 + h:9 + h, :]
                bc_row = rows[28 + h:29 + h, :]
                m0 = m_scr[h][0:1, 0:1]
                d = jnp.where(tril, bc_col - bc_row + ig_row, -jnp.inf)
                inter = bc_col + m0
                m_row = jnp.maximum(jnp.max(d, axis=1, keepdims=True), inter)
                s_in = jnp.exp(inter - m_row)
                sq = _dot_nt(qm, k2) * jnp.exp(d - m_row)
                num_aug = _dot(sq.astype(BF16), v_aug.astype(BF16)) + s_in * _dot(qm, c_pair_b)
                den = num_aug[:, DV_M:DV_M + 1]
                hv = num_aug[:, :DV_M] / jnp.maximum(jnp.abs(den), jnp.exp(-m_row))
                h_ref[pl.ds(t0, ln), h * DV_M:(h + 1) * DV_M] = hv
                b_last = bc_col[ln - 1:ln, :]
                dec = b_last - bc_col + ig_col
                m_new = jnp.maximum(b_last + m0, jnp.max(dec, axis=0, keepdims=True))
                ws = jnp.exp(dec - m_new)
                s0 = jnp.exp(b_last + m0 - m_new)
                upd = s0 * c_pair + _dot_tn(k2, (ws * v_aug).astype(BF16))
                rows_own = (srow < DK_M) if hh == 0 else (srow >= DK_M)
                new_pair = jnp.where(rows_own, upd, new_pair)
                m_scr[h] = jnp.broadcast_to(m_new, (8, LANES))
            c_scr[p] = new_pair
        return 0

    lax.fori_loop(0, n_chunks, chunk, 0)
    cst_ref[...] = c_scr[...]
    mst_ref[...] = m_scr[...]


def _mlstm_prompt(mq, mk, mv, cols, ccols, rows, g):
    s = mq.shape[0]
    ln = M_CHUNK
    tb = g * ln
    rows_ch = rows.reshape(32, s // ln, ln).transpose(1, 0, 2)
    rowb = lambda n: pl.BlockSpec((tb, n), lambda i: (i, 0))
    h, cst, mst = pl.pallas_call(
        functools.partial(_mlstm_prompt_kernel, n_chunks=g, ln=ln), grid=(s // tb,),
        in_specs=[rowb(256), rowb(256), rowb(512), rowb(LANES), rowb(LANES),
                  pl.BlockSpec((g, 32, ln), lambda i: (i, 0, 0))],
        out_specs=[rowb(512), pl.BlockSpec((2, LANES, 2 * LANES), lambda i: (0, 0, 0)),
                   pl.BlockSpec((H_M, 8, LANES), lambda i: (0, 0, 0))],
        out_shape=[jax.ShapeDtypeStruct((s, 512), F32), jax.ShapeDtypeStruct((2, LANES, 2 * LANES), F32),
                   jax.ShapeDtypeStruct((H_M, 8, LANES), F32)],
        scratch_shapes=[pltpu.VMEM((2, LANES, 2 * LANES), F32), pltpu.VMEM((H_M, 8, LANES), F32)],
        compiler_params=_cparams("arbitrary"), name="mlstm_prompt",
    )(mq, mk, mv, cols, ccols, rows_ch)
    c_new = cst[:, :, :DV_M].reshape(H_M, DK_M, DV_M)
    n_new = cst[:, :, DV_M].reshape(H_M, DK_M)
    m_new = mst[:, 0, 0]
    return h, c_new, n_new, m_new


def _mlstm_step_kernel(q_ref, k_ref, v_ref, cols_ref, c_ref, n_ref, m_ref, h_ref, cn_ref, nn_ref, mn_ref):
    cols = cols_ref[...]
    lane = lax.broadcasted_iota(I32, (1, LANES), 1)
    lane8 = lax.broadcasted_iota(I32, (8, LANES), 1)
    row8 = lax.broadcasted_iota(I32, (8, LANES), 0)
    srow = lax.broadcasted_iota(I32, (LANES, 1), 0)
    m_out = jnp.zeros((1, LANES), F32)
    for p in range(H_M // 2):
        q2 = q_ref[:, p * LANES:(p + 1) * LANES]
        k2 = k_ref[:, p * LANES:(p + 1) * LANES]
        q2r = q2.astype(BF16).astype(F32)
        k2r = k2.astype(BF16).astype(F32)
        n_pair = n_ref[:, p * LANES:(p + 1) * LANES]
        c_pair = c_ref[p * LANES:(p + 1) * LANES, :]
        c_pair_b = c_pair.astype(BF16)
        upd = jnp.zeros((LANES, LANES), F32)
        s0_rows = jnp.zeros((LANES, 1), F32)
        n_new = jnp.zeros((1, LANES), F32)
        for hh in range(2):
            h = 2 * p + hh
            own = (lane < DK_M) if hh == 0 else (lane >= DK_M)
            own8 = (lane8 < DK_M) if hh == 0 else (lane8 >= DK_M)
            qf = jnp.where(own, q2r, 0.0)
            v = v_ref[:, h * DV_M:(h + 1) * DV_M]
            ig = cols[:, 8 + h:9 + h]
            lf = cols[:, 12 + h:13 + h]
            m0 = m_ref[:, h:h + 1]
            inter = lf + m0
            m_row = jnp.maximum(ig, inter)
            s_in = jnp.exp(inter - m_row)
            wqk = jnp.sum(qf * k2r, axis=1, keepdims=True) * jnp.exp(ig - m_row)
            q_c = _dot(jnp.broadcast_to(qf, (8, LANES)).astype(BF16), c_pair_b)[0:1, :]
            q_n = jnp.sum(qf * n_pair.astype(BF16).astype(F32), axis=1, keepdims=True)
            num = wqk * v.astype(BF16).astype(F32) + s_in * q_c
            den = wqk + s_in * q_n
            h_ref[:, h * DV_M:(h + 1) * DV_M] = num / jnp.maximum(jnp.abs(den), jnp.exp(-m_row))
            ws = jnp.exp(ig - m_row)
            km8 = jnp.where(row8 == 0, jnp.where(own8, jnp.broadcast_to(k2r, (8, LANES)), 0.0), 0.0).astype(BF16)
            wv8 = jnp.where(row8 == 0, jnp.broadcast_to(ws * v, (8, LANES)), 0.0).astype(BF16)
            upd = upd + _dot_tn(km8, wv8)
            s0_rows = jnp.where((srow < DK_M) if hh == 0 else (srow >= DK_M), s_in, s0_rows)
            n_new = jnp.where(own, s_in * n_pair + ws * k2, n_new)
            m_out = jnp.where(lane == h, m_row, m_out)
        cn_ref[p * LANES:(p + 1) * LANES, :] = s0_rows * c_pair + upd
        nn_ref[:, p * LANES:(p + 1) * LANES] = n_new
    mn_ref[...] = m_out


def _mlstm_step(mq, mk, mv, cols, c0, n0, m0):
    bsz = mq.shape[0]
    b3 = lambda r, n: pl.BlockSpec((None, r, n), lambda i: (i, 0, 0))
    h, cn, nn, mn = pl.pallas_call(
        _mlstm_step_kernel, grid=(bsz,),
        in_specs=[b3(1, 256), b3(1, 256), b3(1, 512), b3(1, LANES), b3(H_M * DK_M, DV_M), b3(1, 256), b3(1, H_M)],
        out_specs=[b3(1, 512), b3(H_M * DK_M, DV_M), b3(1, 256), b3(1, LANES)],
        out_shape=[jax.ShapeDtypeStruct((bsz, 1, 512), F32), jax.ShapeDtypeStruct((bsz, H_M * DK_M, DV_M), F32),
                   jax.ShapeDtypeStruct((bsz, 1, 256), F32), jax.ShapeDtypeStruct((bsz, 1, LANES), F32)],
        compiler_params=_cparams("arbitrary"), name="mlstm_step",
    )(mq[:, None, :], mk[:, None, :], mv[:, None, :], cols[:, None, :],
      c0.reshape(bsz, H_M * DK_M, DV_M), n0.reshape(bsz, 1, H_M * DK_M), m0.reshape(bsz, 1, H_M))
    return (h[:, 0, :], cn.reshape(bsz, H_M, DK_M, DV_M), nn.reshape(bsz, H_M, DK_M), mn[:, 0, :H_M])


def _merge_kernel(x_ref, cy_ref, fy_ref, mh_ref, mo_ref, wg_ref, bg_ref, gm_ref, wb_ref, wo_ref, g_ref, b_ref, o_ref):
    x = x_ref[...]
    gates = jax.nn.sigmoid(_dot(x.astype(BF16), wg_ref[...]) + bg_ref[...])
    segs = []
    for h in range(H_M):
        seg = mh_ref[:, h * DV_M:(h + 1) * DV_M]
        segs.append(seg * lax.rsqrt(jnp.mean(seg * seg, axis=-1, keepdims=True) + LN_EPS))
    my = (jnp.concatenate(segs, axis=1) * gm_ref[...] * mo_ref[...]).astype(BF16)
    mix = gates[:, :D_MODEL] * _dot(cy_ref[...], wb_ref[0])
    mix = mix + gates[:, D_MODEL:2 * D_MODEL] * _dot(fy_ref[...], wb_ref[1])
    mix = mix + gates[:, 2 * D_MODEL:] * _dot(my, wb_ref[2])
    y = _dot(mix.astype(BF16), wo_ref[...])
    o_ref[...] = _layer_norm(ALPHA * x + y, g_ref[...], b_ref[...])


def _merge(x, conv_y, fox_y, m_h, m_o, w, tm):
    r = x.shape[0]
    rowb = lambda n: pl.BlockSpec((tm, n), lambda i: (i, 0))
    full = lambda a: pl.BlockSpec(a.shape, lambda i: (0,) * a.ndim)
    consts = [w['wg'], w['bg'], w['g_mnorm'], w['w_branch'], w['w_mix_out'], w['g_ln1'], w['b_ln1']]
    return pl.pallas_call(
        _merge_kernel, grid=(r // tm,),
        in_specs=[rowb(D_MODEL), rowb(512), rowb(512), rowb(512), rowb(512)] + [full(a) for a in consts],
        out_specs=rowb(D_MODEL), out_shape=jax.ShapeDtypeStruct((r, D_MODEL), F32),
        compiler_params=_cparams("arbitrary"), name="merge",
    )(x, conv_y, fox_y, m_h, m_o, *consts)


def _memkv_kernel(mem_ref, wk_ref, wv_ref, k_ref, v_ref, kb_ref, vb_ref):
    mb = mem_ref[...].astype(BF16)
    k = _dot(mb, wk_ref[...])
    v = _dot(mb, wv_ref[...])
    k_ref[...] = k
    v_ref[...] = v
    kb_ref[...] = k.astype(BF16)
    vb_ref[...] = v.astype(BF16)


def _memkv(mem, wk, wv):
    n = mem.shape[0]
    full = lambda a: pl.BlockSpec(a.shape, lambda i: (0,) * a.ndim)
    sd = lambda dt: jax.ShapeDtypeStruct((n, D_MODEL), dt)
    return pl.pallas_call(
        _memkv_kernel, grid=(1,),
        in_specs=[full(mem), full(wk), full(wv)],
        out_specs=[pl.BlockSpec((n, D_MODEL), lambda i: (0, 0))] * 4,
        out_shape=[sd(F32), sd(F32), sd(BF16), sd(BF16)],
        compiler_params=_cparams("arbitrary"), name="memkv",
    )(mem, wk, wv)


def _xattn_prompt_kernel(x_ref, kb_ref, vb_ref, wq_ref, wo_ref, g_ref, b_ref, o_ref):
    x = x_ref[...]
    q = (_dot(x.astype(BF16), wq_ref[...]) * DH_X ** -0.5).astype(BF16)
    outs = []
    for h in range(H_X):
        sl = slice(h * DH_X, (h + 1) * DH_X)
        s = _dot_nt(q[:, sl], kb_ref[:, sl])
        s = s - jnp.max(s, axis=1, keepdims=True)
        p = jnp.exp(s)
        o = _dot(p.astype(BF16), vb_ref[:, sl]) / jnp.sum(p, axis=1, keepdims=True)
        outs.append(o.astype(BF16))
    y = _dot(jnp.concatenate(outs, axis=1), wo_ref[...])
    o_ref[...] = _layer_norm(ALPHA * x + y, g_ref[...], b_ref[...])


def _xattn_prompt(x, kb, vb, w, tm):
    r = x.shape[0]
    rowb = pl.BlockSpec((tm, D_MODEL), lambda i: (i, 0))
    full = lambda a: pl.BlockSpec(a.shape, lambda i: (0,) * a.ndim)
    consts = [kb, vb, w['w_xq'], w['w_xo'], w['g_ln2'], w['b_ln2']]
    return pl.pallas_call(
        _xattn_prompt_kernel, grid=(r // tm,),
        in_specs=[rowb] + [full(a) for a in consts],
        out_specs=rowb, out_shape=jax.ShapeDtypeStruct((r, D_MODEL), F32),
        compiler_params=_cparams("arbitrary"), name="xattn_prompt",
    )(x, *consts)


def _linear_kernel(x_ref, w_ref, o_ref, *, scale):
    o_ref[...] = (_dot(x_ref[...].astype(BF16), w_ref[...]) * scale).astype(o_ref.dtype)


def _linear(x, w, scale, out_dtype):
    r, n = x.shape[0], w.shape[1]
    full = lambda a: pl.BlockSpec(a.shape, lambda i: (0,) * a.ndim)
    return pl.pallas_call(
        functools.partial(_linear_kernel, scale=scale), grid=(1,),
        in_specs=[full(x), full(w)], out_specs=pl.BlockSpec((r, n), lambda i: (0, 0)),
        out_shape=jax.ShapeDtypeStruct((r, n), out_dtype),
        compiler_params=_cparams("arbitrary"), name="linear",
    )(x, w)


def _linear_ln_kernel(x_ref, a_ref, w_ref, g_ref, b_ref, o_ref):
    y = _dot(a_ref[...].astype(BF16), w_ref[...])
    o_ref[...] = _layer_norm(ALPHA * x_ref[...] + y, g_ref[...], b_ref[...])


def _linear_ln(x, a, w, g, b):
    full = lambda t: pl.BlockSpec(t.shape, lambda i: (0,) * t.ndim)
    return pl.pallas_call(
        _linear_ln_kernel, grid=(1,),
        in_specs=[full(x), full(a), full(w), full(g), full(b)],
        out_specs=pl.BlockSpec(x.shape, lambda i: (0, 0)),
        out_shape=jax.ShapeDtypeStruct(x.shape, F32),
        compiler_params=_cparams("arbitrary"), name="linear_ln",
    )(x, a, w, g, b)


def _xattn_decode_kernel(q_ref, k_ref, v_ref, o_ref):
    w = q_ref.shape[-1]
    row = lax.broadcasted_iota(I32, (8, w), 0)
    lane = lax.broadcasted_iota(I32, (8, w), 1)
    own = (lane // DH_X) == row
    qrows = jnp.where(own, jnp.broadcast_to(q_ref[...].astype(F32), (8, w)), 0.0).astype(BF16)
    s = _dot_nt(qrows, k_ref[...].astype(BF16))
    s = s - jnp.max(s, axis=1, keepdims=True)
    p = jnp.exp(s)
    o = _dot(p.astype(BF16), v_ref[...].astype(BF16)) / jnp.sum(p, axis=1, keepdims=True)
    o_ref[...] = jnp.sum(jnp.where(own, o, 0.0), axis=0, keepdims=True)


def _xattn_decode(q, mem_k, mem_v):
    bsz, n_mem = mem_k.shape[0], mem_k.shape[1]
    b3 = lambda r: pl.BlockSpec((None, r, D_MODEL), lambda i: (i, 0, 0))
    out = pl.pallas_call(
        _xattn_decode_kernel, grid=(bsz,),
        in_specs=[b3(1), b3(n_mem), b3(n_mem)], out_specs=b3(1),
        out_shape=jax.ShapeDtypeStruct((bsz, 1, D_MODEL), F32),
        compiler_params=_cparams("arbitrary"), name="xattn_decode",
    )(q[:, None, :], mem_k.reshape(bsz, n_mem, D_MODEL), mem_v.reshape(bsz, n_mem, D_MODEL))
    return out[:, 0, :]


def _router_kernel(x_ref, w_ref, b_ref, lst_ref, ti_ref, tg_ref, cnt_ref, carry_ref):
    i = pl.program_id(0)

    @pl.when(i == 0)
    def _():
        carry_ref[...] = jnp.zeros_like(carry_ref)

    x1, x2, x3 = _split3(x_ref[...])
    w1, w2, w3 = w_ref[0], w_ref[1], w_ref[2]
    logits = (_dot(x1, w1) + (_dot(x1, w2) + _dot(x2, w1)) + (_dot(x1, w3) + _dot(x2, w2) + _dot(x3, w1))) + b_ref[...]
    tm = logits.shape[0]
    lane = lax.broadcasted_iota(I32, (tm, LANES), 1)
    lane_f = lane.astype(F32)
    vals = logits
    tops, idxs, sels = [], [], []
    for _ in range(TOP_K):
        mx = jnp.max(vals, axis=1, keepdims=True)
        idx = jnp.min(jnp.where(vals == mx, lane_f, float(LANES)), axis=1, keepdims=True)
        sel = lane_f == idx
        tops.append(mx)
        idxs.append(idx)
        sels.append(sel)
        vals = jnp.where(sel, -jnp.inf, vals)
    exps = [jnp.exp(t - tops[0]) for t in tops]
    den = exps[0] + exps[1] + exps[2] + exps[3]
    cnt = jnp.zeros((tm, LANES), F32)
    for sel in sels:
        cnt = cnt + jnp.where(sel, 1.0, 0.0)
    excl = _dot(lst_ref[...], cnt.astype(BF16)) + carry_ref[0:1, :]
    ti = jnp.zeros((tm, LANES), F32)
    tg = jnp.zeros((tm, LANES), F32)
    for k in range(TOP_K):
        rank = jnp.sum(jnp.where(sels[k], excl, 0.0), axis=1, keepdims=True)
        ti = jnp.where(lane == k, idxs[k], ti)
        ti = jnp.where(lane == TOP_K + k, rank, ti)
        tg = jnp.where(lane == k, exps[k] / den, tg)
    ti_ref[...] = ti.astype(I32)
    tg_ref[...] = tg
    carry_ref[...] = carry_ref[...] + jnp.sum(cnt, axis=0, keepdims=True)
    cnt_ref[...] = carry_ref[...]


def _router(x, w3, b, tm):
    r = x.shape[0]
    t = np.arange(tm)
    lst = jnp.asarray(t[:, None] > t[None, :], BF16)
    full = lambda a: pl.BlockSpec(a.shape, lambda i: (0,) * a.ndim)
    rowb = lambda n: pl.BlockSpec((tm, n), lambda i: (i, 0))
    return pl.pallas_call(
        _router_kernel, grid=(r // tm,),
        in_specs=[rowb(D_MODEL), full(w3), full(b), full(lst)],
        out_specs=[rowb(LANES), rowb(LANES), pl.BlockSpec((8, LANES), lambda i: (0, 0))],
        out_shape=[jax.ShapeDtypeStruct((r, LANES), I32), jax.ShapeDtypeStruct((r, LANES), F32),
                   jax.ShapeDtypeStruct((8, LANES), F32)],
        scratch_shapes=[pltpu.VMEM((8, LANES), F32)],
        compiler_params=_cparams("arbitrary"), name="router",
    )(x, w3, b, lst)


def _moe_ffn_kernel(be_ref, nv_ref, idx_hbm, x_hbm, wgu_ref, bgu_ref, wd_ref, bd_ref, y_hbm,
                    idx_smem, xbuf, ybuf, wgu_b, wd_b, sem_idx, sem_in, sem_out, *, tb):
    i = pl.program_id(0)

    @pl.when(i < nv_ref[0])
    def _():
        idx_cp = pltpu.make_async_copy(idx_hbm.at[i], idx_smem, sem_idx)
        idx_cp.start()
        e = be_ref[i]
        e_prev = be_ref[jnp.maximum(i - 1, 0)]

        @pl.when((i == 0) | (e != e_prev))
        def _():
            wgu_b[...] = wgu_ref[...].astype(BF16)
            wd_b[...] = wd_ref[...].astype(BF16)

        idx_cp.wait()

        def gather(r, _):
            pltpu.make_async_copy(x_hbm.at[pl.ds(idx_smem[r], 1)], xbuf.at[pl.ds(r, 1)], sem_in).start()
            return 0

        lax.fori_loop(0, tb, gather, 0)
        pltpu.make_async_copy(x_hbm.at[pl.ds(0, tb)], xbuf, sem_in).wait()

        gu = _dot(xbuf[...].astype(BF16), wgu_b[...]) + bgu_ref[...]
        g = jnp.minimum(gu[:, :D_EXPERT], SWIGLU_LIMIT)
        u = jnp.clip(gu[:, D_EXPERT:], -SWIGLU_LIMIT, SWIGLU_LIMIT)
        act = (u + 1.0) * g * jax.nn.sigmoid(SWIGLU_ALPHA * g)
        ybuf[...] = _dot(act.astype(BF16), wd_b[...]) + bd_ref[...]

        n_rows = idx_smem[2 * tb]

        def scatter(r, _):
            pltpu.make_async_copy(ybuf.at[pl.ds(r, 1)], y_hbm.at[pl.ds(idx_smem[tb + r], 1)], sem_out).start()
            return 0

        lax.fori_loop(0, n_rows, scatter, 0)
        n_al = pl.multiple_of((n_rows // 8) * 8, 8)

        @pl.when(n_al > 0)
        def _():
            pltpu.make_async_copy(ybuf.at[pl.ds(0, n_al)], y_hbm.at[pl.ds(0, n_al)], sem_out).wait()

        def wait_row(r, _):
            pltpu.make_async_copy(ybuf.at[pl.ds(0, 1)], y_hbm.at[pl.ds(0, 1)], sem_out).wait()
            return 0

        lax.fori_loop(n_al, n_rows, wait_row, 0)


def _moe(x, w, l, tb, tm_r, tm_c):
    n = x.shape[0]
    a = n * TOP_K
    ti, tg, cnt = _router(x, w['w_router3'], w['b_router'], tm_r)
    top_e = ti[:, :TOP_K]
    rank = ti[:, TOP_K:2 * TOP_K]
    counts = cnt[0, :N_EXPERTS].astype(I32)
    n_blk = -(-a // tb) + N_EXPERTS
    rows = n_blk * tb
    padded = (counts + tb - 1) // tb * tb
    pend = jnp.cumsum(padded)
    dest = (pend - padded)[top_e] + rank
    tok = jnp.broadcast_to(jnp.arange(n, dtype=I32)[:, None], (n, TOP_K))
    pair = jnp.arange(TOP_K, dtype=I32)[None, :] * n + tok
    row_tok = jnp.zeros((rows,), I32).at[dest.reshape(-1)].set(tok.reshape(-1))
    row_dst = jnp.zeros((rows,), I32).at[dest.reshape(-1)].set(pair.reshape(-1))
    row_ok = jnp.zeros((rows,), I32).at[dest.reshape(-1)].set(1)
    n_rows = jnp.broadcast_to(jnp.sum(row_ok.reshape(n_blk, tb), axis=1, keepdims=True), (n_blk, 8))
    idx = jnp.concatenate([row_tok.reshape(n_blk, tb), row_dst.reshape(n_blk, tb), n_rows], axis=1)
    n_valid = (pend[-1] // tb).astype(I32)
    blk_id = jnp.minimum(jnp.arange(n_blk, dtype=I32), n_valid - 1)
    blk_e = jnp.minimum(jnp.searchsorted(pend, blk_id * tb, side='right'), N_EXPERTS - 1).astype(I32)

    grid_spec = pltpu.PrefetchScalarGridSpec(
        num_scalar_prefetch=2, grid=(n_blk,),
        in_specs=[pl.BlockSpec(memory_space=pl.ANY), pl.BlockSpec(memory_space=pl.ANY),
                  pl.BlockSpec((None, None, D_MODEL, 2 * D_EXPERT), lambda i, be, nv: (l, be[i], 0, 0)),
                  pl.BlockSpec((None, None, 1, 2 * D_EXPERT), lambda i, be, nv: (l, be[i], 0, 0)),
                  pl.BlockSpec((None, None, D_EXPERT, D_MODEL), lambda i, be, nv: (l, be[i], 0, 0)),
                  pl.BlockSpec((None, None, 1, D_MODEL), lambda i, be, nv: (l, be[i], 0, 0))],
        out_specs=pl.BlockSpec(memory_space=pl.ANY),
        scratch_shapes=[pltpu.SMEM((2 * tb + 8,), I32), pltpu.VMEM((tb, D_MODEL), F32), pltpu.VMEM((tb, D_MODEL), F32),
                        pltpu.VMEM((D_MODEL, 2 * D_EXPERT), BF16), pltpu.VMEM((D_EXPERT, D_MODEL), BF16),
                        pltpu.SemaphoreType.DMA(()), pltpu.SemaphoreType.DMA(()), pltpu.SemaphoreType.DMA(())])
    y_pairs = pl.pallas_call(
        functools.partial(_moe_ffn_kernel, tb=tb), grid_spec=grid_spec,
        out_shape=jax.ShapeDtypeStruct((a, D_MODEL), F32),
        compiler_params=_cparams("arbitrary"), name="moe_ffn",
    )(blk_e, n_valid[None], idx, x, w['w_gate_up'], w['b_gate_up'][:, :, None, :], w['w_down'],
      w['b_down'][:, :, None, :])
    return _combine(x, y_pairs, tg, w['g_ln3'], w['b_ln3'], tm_c)


def _combine_kernel(x_ref, y0_ref, y1_ref, y2_ref, y3_ref, tg_ref, g_ref, b_ref, o_ref):
    tg = tg_ref[...]
    ff = tg[:, 0:1] * y0_ref[...]
    for k, yr in enumerate((y1_ref, y2_ref, y3_ref), start=1):
        ff = ff + tg[:, k:k + 1] * yr[...]
    o_ref[...] = _layer_norm(ALPHA * x_ref[...] + ff, g_ref[...], b_ref[...])


def _combine(x, y_pairs, tg, g, b, tm):
    n = x.shape[0]
    nb = n // tm
    rowb = lambda w: pl.BlockSpec((tm, w), lambda i: (i, 0))
    ysp = lambda k: pl.BlockSpec((tm, D_MODEL), lambda i: (k * nb + i, 0))
    full = lambda t: pl.BlockSpec(t.shape, lambda i: (0,) * t.ndim)
    return pl.pallas_call(
        _combine_kernel, grid=(nb,),
        in_specs=[rowb(D_MODEL), ysp(0), ysp(1), ysp(2), ysp(3), rowb(LANES), full(g), full(b)],
        out_specs=rowb(D_MODEL), out_shape=jax.ShapeDtypeStruct((n, D_MODEL), F32),
        compiler_params=_cparams("arbitrary"), name="moe_combine",
    )(x, y_pairs, y_pairs, y_pairs, y_pairs, tg, g, b)


def _layer_weights(l, w_in, b_in, b_fox_f, b_mlstm_f, g_mlstm_norm, w_branch, w_mix_out, g_ln1, b_ln1,
                   w_xq, w_xk, w_xv, w_xo, g_ln2, b_ln2, w_router, b_router, w_gate_up, b_gate_up,
                   w_down, b_down, g_ln3, b_ln3):
    w = _split_w_in(w_in[l], b_in[l], b_fox_f[l], b_mlstm_f[l])
    wr = jnp.pad(w_router[l], ((0, 0), (0, LANES - N_EXPERTS)))
    r1 = wr.astype(BF16)
    r2 = (wr - r1.astype(F32)).astype(BF16)
    r3 = (wr - r1.astype(F32) - r2.astype(F32)).astype(BF16)
    w.update(
        g_mnorm=g_mlstm_norm[l][None], w_branch=w_branch[l].astype(BF16), w_mix_out=w_mix_out[l].astype(BF16),
        g_ln1=g_ln1[l][None], b_ln1=b_ln1[l][None],
        w_xq=w_xq[l].astype(BF16), w_xk=w_xk[l].astype(BF16), w_xv=w_xv[l].astype(BF16), w_xo=w_xo[l].astype(BF16),
        g_ln2=g_ln2[l][None], b_ln2=b_ln2[l][None],
        w_router3=jnp.stack([r1, r2, r3]),
        b_router=jnp.pad(b_router[l], (0, LANES - N_EXPERTS), constant_values=NEG_BIG)[None],
        w_gate_up=w_gate_up, b_gate_up=b_gate_up, w_down=w_down, b_down=b_down,
        g_ln3=g_ln3[l][None], b_ln3=b_ln3[l][None])
    return w


def kernel(x_prompt, x_sample, mem_prompt, cache_fox_k, cache_fox_v, cache_fox_lf, page_table, state_conv,
           state_mlstm_c, state_mlstm_n, state_mlstm_m, cache_mem_k, cache_mem_v, w_in, b_in, b_fox_f, b_mlstm_f,
           w_dw, b_dw, g_conv_ln, b_conv_ln, g_mlstm_norm, w_branch, w_mix_out, g_ln1, b_ln1, w_xq, w_xk, w_xv,
           w_xo, g_ln2, b_ln2, w_router, b_router, w_gate_up, b_gate_up, w_down, b_down, g_ln3, b_ln3):
    b_p, s, _ = x_prompt.shape
    b_s = x_sample.shape[0]
    assert b_p == 1 and x_sample.shape[1] == 1
    depth = w_in.shape[0]
    xp = x_prompt.reshape(s, D_MODEL)
    xs = x_sample.reshape(b_s, D_MODEL)
    mem = mem_prompt.reshape(-1, D_MODEL)
    tm = _row_block(s, 256)
    tq = _row_block(s, 256)
    tk = _row_block(s, 512)
    g_chunks = 8 if s % (8 * M_CHUNK) == 0 else 1
    tb_p = 256 if s * TOP_K >= 256 * N_EXPERTS else 8
    rows_p, rows_s = [], []
    for l in range(depth):
        w = _layer_weights(l, w_in, b_in, b_fox_f, b_mlstm_f, g_mlstm_norm, w_branch, w_mix_out, g_ln1, b_ln1,
                           w_xq, w_xk, w_xv, w_xo, g_ln2, b_ln2, w_router, b_router, w_gate_up, b_gate_up,
                           w_down, b_down, g_ln3, b_ln3)
        pr = _inproj(xp, w, tm)
        conv_y = _conv_prompt(pr['u'], w_dw[l], b_dw[l], g_conv_ln[l], b_conv_ln[l], tm)
        fox_y = _fox_prompt(pr['fqb'], pr['fkb'], pr['fvb'], pr['rows'][16:24], tq, tk)
        m_h, m_c, m_n, m_m = _mlstm_prompt(pr['mq'], pr['mk'], pr['mv'], pr['cols'], pr['ccols'], pr['rows'], g_chunks)
        xp = _merge(xp, conv_y, fox_y, m_h, pr['mo'], w, tm)
        mem_k, mem_v, mem_kb, mem_vb = _memkv(mem, w['w_xk'], w['w_xv'])
        xp = _xattn_prompt(xp, mem_kb, mem_vb, w, tm)
        xp = _moe(xp, w, l, tb_p, tm, tm)
        rows_p.append((pr['fk'].reshape(1, s, H_F, DH_F), pr['fv'].reshape(1, s, H_F, DH_F),
                       pr['cols'][:, :H_F].reshape(1, s, H_F), pr['u'][s - (K_CONV - 1):][None],
                       m_c[None], m_n[None], m_m[None],
                       mem_k.reshape(1, -1, H_X, DH_X), mem_v.reshape(1, -1, H_X, DH_X)))
        sr = _inproj(xs, w, b_s)
        conv_y, conv_st = _conv_step(state_conv[l], sr['u'], w_dw[l], b_dw[l], g_conv_ln[l], b_conv_ln[l])
        lf_new = sr['cols'][:, :H_F]
        fox_y = _fox_decode(page_table, sr['fqb'], sr['fk'], sr['fv'], lf_new, cache_fox_k, cache_fox_v,
                            cache_fox_lf, l)
        m_h, m_c, m_n, m_m = _mlstm_step(sr['mq'], sr['mk'], sr['mv'], sr['cols'], state_mlstm_c[l],
                                         state_mlstm_n[l], state_mlstm_m[l])
        xs = _merge(xs, conv_y, fox_y.astype(BF16), m_h, sr['mo'], w, b_s)
        q = _linear(xs, w['w_xq'], DH_X ** -0.5, BF16)
        att = _xattn_decode(q, cache_mem_k[l], cache_mem_v[l])
        xs = _linear_ln(xs, att, w['w_xo'], w['g_ln2'], w['b_ln2'])
        xs = _moe(xs, w, l, 8, b_s, b_s)
        rows_s.append((sr['fk'].reshape(b_s, 1, H_F, DH_F), sr['fv'].reshape(b_s, 1, H_F, DH_F),
                       lf_new.reshape(b_s, 1, H_F), conv_st, m_c, m_n, m_m))
    outs_p = [jnp.stack(a) for a in zip(*rows_p)]
    outs_s = [jnp.stack(a) for a in zip(*rows_s)]
    return (xp.reshape(1, s, D_MODEL), xs.reshape(b_s, 1, D_MODEL), *outs_p, *outs_s)
```

```python
import functools

import numpy as np
import jax
import jax.numpy as jnp
from jax import lax
from jax.experimental import pallas as pl
from jax.experimental.pallas import tpu as pltpu

F32 = jnp.float32
BF16 = jnp.bfloat16
I32 = jnp.int32

D_MODEL = 1024
W_BR = 512
K_CONV = 31
H_F = 8
DH_F = 64
H_M = 4
DK_M = 64
DV_M = 128
M_CHUNK = 64
I_CAP = 15.0
H_X = 4
DH_X = 256
N_EXPERTS = 32
TOP_K = 4
D_EXPERT = 1024
SWIGLU_ALPHA = 1.702
SWIGLU_LIMIT = 7.0
LN_EPS = 1e-5
DEPTH = 2
ALPHA = (2.0 * DEPTH) ** 0.25
NEG_BIG = -1e30
LANES = 128
VMEM_LIMIT = 56 * 1024 * 1024


def _cparams(*sem):
    return pltpu.CompilerParams(dimension_semantics=sem, vmem_limit_bytes=VMEM_LIMIT)


def _dot(a, b):
    return jnp.dot(a, b, preferred_element_type=F32)


def _dot_nt(a, b):
    return lax.dot_general(a, b, (((1,), (1,)), ((), ())), preferred_element_type=F32)


def _dot_tn(a, b):
    return lax.dot_general(a, b, (((0,), (0,)), ((), ())), preferred_element_type=F32)


def _split3(a):
    a1 = a.astype(BF16)
    r = a - a1.astype(F32)
    a2 = r.astype(BF16)
    r = r - a2.astype(F32)
    return a1, a2, r.astype(BF16)


def _log_sigmoid(x):
    return jnp.minimum(x, 0.0) - jnp.log1p(jnp.exp(-jnp.abs(x)))


def _layer_norm(x, g, b):
    mu = jnp.mean(x, axis=-1, keepdims=True)
    xc = x - mu
    var = jnp.mean(xc * xc, axis=-1, keepdims=True)
    return xc * lax.rsqrt(var + LN_EPS) * g + b


def _row_block(n, pref):
    return pref if n % pref == 0 else n


def _gate_act(z, idx):
    ls = _log_sigmoid(z)
    tg = I_CAP * jnp.tanh(z * (1.0 / I_CAP))
    return jnp.where(idx < 8, ls, jnp.where(idx < 12, tg, jnp.where(idx < 16, ls, 0.0)))


def _inproj_kernel(x_ref, wc_ref, bc_ref, wf_ref, bf_ref, wm_ref, bm_ref, ws_ref, bs_ref, wst_ref, bst_ref,
                   ufull_ref, ublk_ref, lblk_ref,
                   u_ref, fqb_ref, fk_ref, fv_ref, fkb_ref, fvb_ref, mq_ref, mk_ref, mv_ref, mo_ref,
                   cols_ref, ccols_ref, rows_ref, carry_ref):
    i = pl.program_id(0)

    @pl.when(i == 0)
    def _():
        carry_ref[...] = jnp.zeros_like(carry_ref)

    tm = x_ref.shape[0]
    xb = x_ref[...].astype(BF16)
    zc = _dot(xb, wc_ref[...]) + bc_ref[...]
    u_ref[...] = zc[:, :W_BR] * jax.nn.sigmoid(zc[:, W_BR:])

    zf = _dot(xb, wf_ref[...]) + bf_ref[...]
    fk = zf[:, W_BR:2 * W_BR]
    fv = zf[:, 2 * W_BR:]
    fqb_ref[...] = (zf[:, :W_BR] * DH_F ** -0.5).astype(BF16)
    fk_ref[...] = fk
    fv_ref[...] = fv
    fkb_ref[...] = fk.astype(BF16)
    fvb_ref[...] = fv.astype(BF16)

    zm = _dot(xb, wm_ref[...]) + bm_ref[...]
    mq_ref[...] = zm[:, :256]
    mk_ref[...] = zm[:, 256:512] * DK_M ** -0.5
    mv_ref[...] = zm[:, 512:1024]
    mo_ref[...] = jax.nn.sigmoid(zm[:, 1024:])

    zs = _dot(xb, ws_ref[...]) + bs_ref[...]
    lane = lax.broadcasted_iota(I32, zs.shape, 1)
    act_c = _gate_act(zs, lane)
    cols_ref[...] = act_c
    c1, c2, c3 = _split3(act_c)
    lb = lblk_ref[...]
    ccols_ref[...] = _dot(lb, c1) + _dot(lb, c2) + _dot(lb, c3)

    zt = _dot_nt(wst_ref[...], xb) + bst_ref[...]
    row = lax.broadcasted_iota(I32, zt.shape, 0)
    act_r = _gate_act(zt, row)
    r1, r2, r3 = _split3(act_r)
    uf = ufull_ref[...]
    ub = ublk_ref[...]
    cum_full = _dot(r1, uf) + _dot(r2, uf) + _dot(r3, uf) + carry_ref[:, 0:1]
    cum_blk = _dot(r1, ub) + _dot(r2, ub) + _dot(r3, ub)
    carry_ref[...] = jnp.broadcast_to(cum_full[:, tm - 1:tm], carry_ref.shape)
    rows_ref[0:16, :] = act_r
    rows_ref[16:32, :] = jnp.where(row < 8, cum_full, cum_blk)


def _tri_consts(tm, chunk):
    t = np.arange(tm)
    upper = (t[:, None] <= t[None, :])
    same = (t[:, None] // chunk) == (t[None, :] // chunk)
    ufull = jnp.asarray(upper, BF16)
    ublk = jnp.asarray(upper & same, BF16)
    lblk = jnp.asarray(upper.T & same, BF16)
    return ufull, ublk, lblk


def _split_w_in(w_in, b_in, b_fox_f, b_mlstm_f):
    o = np.cumsum([0, 512, 512, 512, 512, 512, 8, 256, 256, 512, 4, 4, 512, 3072])
    sl = lambda a, i, j: a[..., o[i]:o[j]]
    wc, bc = sl(w_in, 0, 2), sl(b_in, 0, 2)
    wf, bf = sl(w_in, 2, 5), sl(b_in, 2, 5)
    wm = jnp.concatenate([sl(w_in, 6, 9), sl(w_in, 11, 12)], axis=-1)
    bm = jnp.concatenate([sl(b_in, 6, 9), sl(b_in, 11, 12)], axis=-1)
    wsm = jnp.concatenate([sl(w_in, 5, 6), sl(w_in, 9, 11)], axis=-1)
    bsm = jnp.concatenate([sl(b_in, 5, 6) + b_fox_f, sl(b_in, 9, 10), sl(b_in, 10, 11) + b_mlstm_f], axis=-1)
    ws = jnp.pad(wsm, ((0, 0), (0, LANES - 16)))
    bs = jnp.pad(bsm, (0, LANES - 16))
    wg, bg = sl(w_in, 12, 13), sl(b_in, 12, 13)
    return dict(wc=wc.astype(BF16), bc=bc[None], wf=wf.astype(BF16), bf=bf[None], wm=wm.astype(BF16), bm=bm[None],
                ws=ws.astype(BF16), bs=bs[None], wst=wsm.T.astype(BF16), bst=bsm[:, None],
                wg=wg.astype(BF16), bg=bg[None])


def _inproj(x, w, tm):
    r = x.shape[0]
    chunk = min(M_CHUNK, tm)
    ufull, ublk, lblk = _tri_consts(tm, chunk)
    rowb = lambda n: pl.BlockSpec((tm, n), lambda i: (i, 0))
    full = lambda a: pl.BlockSpec(a.shape, lambda i: (0,) * a.ndim)
    consts = [w['wc'], w['bc'], w['wf'], w['bf'], w['wm'], w['bm'], w['ws'], w['bs'], w['wst'], w['bst'],
              ufull, ublk, lblk]
    out_shape = [jax.ShapeDtypeStruct((r, 512), F32), jax.ShapeDtypeStruct((r, 512), BF16),
                 jax.ShapeDtypeStruct((r, 512), F32), jax.ShapeDtypeStruct((r, 512), F32),
                 jax.ShapeDtypeStruct((r, 512), BF16), jax.ShapeDtypeStruct((r, 512), BF16),
                 jax.ShapeDtypeStruct((r, 256), F32), jax.ShapeDtypeStruct((r, 256), F32),
                 jax.ShapeDtypeStruct((r, 512), F32), jax.ShapeDtypeStruct((r, 512), F32),
                 jax.ShapeDtypeStruct((r, LANES), F32), jax.ShapeDtypeStruct((r, LANES), F32),
                 jax.ShapeDtypeStruct((32, r), F32)]
    out_specs = [rowb(512)] * 6 + [rowb(256)] * 2 + [rowb(512)] * 2 + [rowb(LANES)] * 2 + \
                [pl.BlockSpec((32, tm), lambda i: (0, i))]
    outs = pl.pallas_call(
        _inproj_kernel, grid=(r // tm,),
        in_specs=[rowb(D_MODEL)] + [full(a) for a in consts],
        out_specs=out_specs, out_shape=out_shape,
        scratch_shapes=[pltpu.VMEM((16, LANES), F32)],
        compiler_params=_cparams("arbitrary"), name="inproj",
    )(x, *consts)
    keys = ['u', 'fqb', 'fk', 'fv', 'fkb', 'fvb', 'mq', 'mk', 'mv', 'mo', 'cols', 'ccols', 'rows']
    return dict(zip(keys, outs))


HALO = 32


def _conv_kernel(u_ref, w_ref, b_ref, g_ref, bln_ref, y_ref, buf_ref):
    i = pl.program_id(0)
    tm = u_ref.shape[0]

    @pl.when(i == 0)
    def _():
        buf_ref[0:HALO, :] = jnp.zeros((HALO, W_BR), F32)

    buf_ref[HALO:HALO + tm, :] = u_ref[...]
    acc = jnp.zeros((tm, W_BR), F32) + b_ref[...]
    off = HALO - (K_CONV - 1)
    for k in range(K_CONV):
        acc = acc + w_ref[k:k + 1, :] * buf_ref[off + k:off + k + tm, :]
    yn = _layer_norm(acc, g_ref[...], bln_ref[...])
    y_ref[...] = (yn * jax.nn.sigmoid(yn)).astype(y_ref.dtype)
    buf_ref[0:HALO, :] = buf_ref[tm:tm + HALO, :]


def _conv_prompt(u, w_dw, b_dw, g, b, tm):
    s = u.shape[0]
    full = lambda a: pl.BlockSpec(a.shape, lambda i: (0,) * a.ndim)
    args = [w_dw, b_dw[None], g[None], b[None]]
    return pl.pallas_call(
        _conv_kernel, grid=(s // tm,),
        in_specs=[pl.BlockSpec((tm, W_BR), lambda i: (i, 0))] + [full(a) for a in args],
        out_specs=pl.BlockSpec((tm, W_BR), lambda i: (i, 0)),
        out_shape=jax.ShapeDtypeStruct((s, W_BR), BF16),
        scratch_shapes=[pltpu.VMEM((HALO + tm, W_BR), F32)],
        compiler_params=_cparams("arbitrary"), name="conv_prompt",
    )(u, *args)


def _conv_step_kernel(st_ref, u_ref, w_ref, b_ref, g_ref, bln_ref, y_ref, nst_ref):
    st = st_ref[0]
    un = u_ref[0]
    acc = b_ref[...] + w_ref[K_CONV - 1:K_CONV, :] * un
    acc = acc + jnp.sum(w_ref[0:K_CONV - 1, :] * st, axis=0, keepdims=True)
    yn = _layer_norm(acc, g_ref[...], bln_ref[...])
    y_ref[0] = (yn * jax.nn.sigmoid(yn)).astype(y_ref.dtype)
    nst_ref[0, 0:K_CONV - 2, :] = st[1:, :]
    nst_ref[0, K_CONV - 2:K_CONV - 1, :] = un


def _conv_step(state, u, w_dw, b_dw, g, b):
    bsz = u.shape[0]
    full = lambda a: pl.BlockSpec(a.shape, lambda i: (0,) * a.ndim)
    args = [w_dw, b_dw[None], g[None], b[None]]
    y, nst = pl.pallas_call(
        _conv_step_kernel, grid=(bsz,),
        in_specs=[pl.BlockSpec((1, K_CONV - 1, W_BR), lambda i: (i, 0, 0)),
                  pl.BlockSpec((1, 1, W_BR), lambda i: (i, 0, 0))] + [full(a) for a in args],
        out_specs=[pl.BlockSpec((1, 1, W_BR), lambda i: (i, 0, 0)),
                   pl.BlockSpec((1, K_CONV - 1, W_BR), lambda i: (i, 0, 0))],
        out_shape=[jax.ShapeDtypeStruct((bsz, 1, W_BR), BF16),
                   jax.ShapeDtypeStruct((bsz, K_CONV - 1, W_BR), F32)],
        compiler_params=_cparams("arbitrary"), name="conv_step",
    )(state, u[:, None, :], *args)
    return y[:, 0, :], nst


def _fox_prompt_kernel(q_ref, k_ref, v_ref, c_ref, o_ref, *, tq, tk):
    qi = pl.program_id(1)
    q2 = q_ref[...].astype(F32)
    lane = lax.broadcasted_iota(I32, (tq, LANES), 1)
    q_start = pl.multiple_of(qi * tq, tq)
    n_full = (qi * tq) // tk
    qms = [jnp.where(lane < DH_F, q2, 0.0).astype(BF16), jnp.where(lane < DH_F, 0.0, q2).astype(BF16)]
    c0s = [c_ref[hh:hh + 1, pl.ds(q_start, LANES)][:, 0:1] for hh in range(2)]

    def step(j, carry, masked):
        k_start = pl.multiple_of(j * tk, tk)
        kb = k_ref[pl.ds(k_start, tk), :]
        vb = v_ref[pl.ds(k_start, tk), :]
        new = []
        for hh in range(2):
            m, l, acc = carry[hh]
            s = _dot_nt(qms[hh], kb) + (c0s[hh] - c_ref[hh:hh + 1, pl.ds(k_start, tk)])
            if masked:
                rows = q_start + lax.broadcasted_iota(I32, (tq, tk), 0)
                cols = k_start + lax.broadcasted_iota(I32, (tq, tk), 1)
                s = jnp.where(cols <= rows, s, NEG_BIG)
            m_new = jnp.maximum(m, jnp.max(s, axis=1, keepdims=True))
            a = jnp.exp(m - m_new)
            p = jnp.exp(s - m_new)
            l = a * l + jnp.sum(p, axis=1, keepdims=True)
            acc = a * acc + _dot(p.astype(BF16), vb)
            new.append((m_new, l, acc))
        return tuple(new)

    one = (jnp.full((tq, 1), NEG_BIG, F32), jnp.zeros((tq, 1), F32), jnp.zeros((tq, LANES), F32))
    carry = lax.fori_loop(0, n_full, functools.partial(step, masked=False), (one, one))
    (_, l0, acc0), (_, l1, acc1) = step(n_full, carry, True)
    o_ref[...] = jnp.where(lane < DH_F, acc0 / l0, acc1 / l1).astype(o_ref.dtype)


def _fox_prompt(fqb, fkb, fvb, crow, tq, tk):
    s = fqb.shape[0]
    c3 = crow.reshape(H_F // 2, 2, s)
    return pl.pallas_call(
        functools.partial(_fox_prompt_kernel, tq=tq, tk=tk), grid=(H_F // 2, s // tq),
        in_specs=[pl.BlockSpec((tq, LANES), lambda p, i: (i, p)),
                  pl.BlockSpec((s, LANES), lambda p, i: (0, p)),
                  pl.BlockSpec((s, LANES), lambda p, i: (0, p)),
                  pl.BlockSpec((None, 2, s), lambda p, i: (p, 0, 0))],
        out_specs=pl.BlockSpec((tq, LANES), lambda p, i: (i, p)),
        out_shape=jax.ShapeDtypeStruct((s, W_BR), BF16),
        compiler_params=_cparams("arbitrary", "arbitrary"), name="fox_prompt",
    )(fqb, fkb, fvb, c3)


def _largest_divisor(n, options):
    return next(o for o in options if n % o == 0)


def _fox_suffix_kernel(pt_ref, *refs, g):
    lf_refs, tri_ref, o_ref, carry_ref = refs[:g], refs[g], refs[g + 1], refs[g + 2]
    page = lf_refs[0].shape[0]

    @pl.when(pl.program_id(1) == 0)
    def _():
        carry_ref[...] = jnp.zeros_like(carry_ref)

    tri = tri_ref[...]
    carry = carry_ref[0:1, 0:H_F]
    for gg in reversed(range(g)):
        x1, x2, x3 = _split3(lf_refs[gg][...])
        incl = _dot(tri, x1) + _dot(tri, x2) + _dot(tri, x3) + carry
        o_ref[gg * page:(gg + 1) * page, :] = incl
        carry = incl[0:1, :]
    carry_ref[0:1, 0:H_F] = carry


def _fox_suffix(page_table, cache_lf, layer, g):
    bsz, n_pages = page_table.shape
    page = cache_lf.shape[2]
    n_steps = n_pages // g
    t = np.arange(page)
    tri = jnp.asarray(t[None, :] >= t[:, None], BF16)

    def page_spec(gg):
        return pl.BlockSpec((None, None, page, H_F),
                            lambda b, s, pt: (layer, pt[b, (n_steps - 1 - s) * g + gg], 0, 0))

    grid_spec = pltpu.PrefetchScalarGridSpec(
        num_scalar_prefetch=1, grid=(bsz, n_steps),
        in_specs=[page_spec(gg) for gg in range(g)] + [pl.BlockSpec((page, page), lambda b, s, pt: (0, 0))],
        out_specs=pl.BlockSpec((None, g * page, H_F), lambda b, s, pt: (b, n_steps - 1 - s, 0)),
        scratch_shapes=[pltpu.VMEM((8, LANES), F32)])
    return pl.pallas_call(
        functools.partial(_fox_suffix_kernel, g=g), grid_spec=grid_spec,
        out_shape=jax.ShapeDtypeStruct((bsz, n_pages * page, H_F), F32),
        compiler_params=_cparams("arbitrary", "arbitrary"), name="fox_suffix",
    )(page_table, *([cache_lf] * g), tri)


def _fox_decode_kernel(pt_ref, *refs, g):
    q_ref, k_refs, v_refs = refs[0], refs[1:1 + g], refs[1 + g:1 + 2 * g]
    bias_ref, kn_ref, vn_ref, o_ref, m_ref, l_ref, acc_ref = refs[1 + 2 * g:]
    step = pl.program_id(1)
    page = k_refs[0].shape[0]
    w = page * H_F

    @pl.when(step == 0)
    def _():
        m_ref[...] = jnp.full_like(m_ref, NEG_BIG)
        l_ref[...] = jnp.zeros_like(l_ref)
        acc_ref[...] = jnp.zeros_like(acc_ref)

    q8 = q_ref[...]
    row = lax.broadcasted_iota(I32, (H_F, w), 0)
    lane = lax.broadcasted_iota(I32, (H_F, w), 1)
    own = jnp.bitwise_and(lane, H_F - 1) == row
    scores = []
    for gg in range(g):
        k2 = k_refs[gg][...].reshape(w, DH_F).astype(BF16)
        sg = _dot_nt(q8, k2) + bias_ref[:, gg * w:(gg + 1) * w]
        scores.append(jnp.where(own, sg, NEG_BIG))
    m_old = m_ref[:, 0:1]
    m_new = m_old
    for sg in scores:
        m_new = jnp.maximum(m_new, jnp.max(sg, axis=1, keepdims=True))
    a = jnp.exp(m_old - m_new)
    l_new = a * l_ref[:, 0:1]
    acc = a * acc_ref[...]
    for gg in range(g):
        pr = jnp.exp(scores[gg] - m_new)
        l_new = l_new + jnp.sum(pr, axis=1, keepdims=True)
        acc = acc + _dot(pr.astype(BF16), v_refs[gg][...].reshape(w, DH_F).astype(BF16))
    m_ref[...] = jnp.broadcast_to(m_new, m_ref.shape)
    l_ref[...] = jnp.broadcast_to(l_new, l_ref.shape)
    acc_ref[...] = acc

    @pl.when(step == pl.num_programs(1) - 1)
    def _():
        kn = kn_ref[...].astype(BF16).astype(F32)
        s_new = jnp.sum(q8.astype(F32) * kn, axis=1, keepdims=True)
        m_fin = jnp.maximum(m_new, s_new)
        a2 = jnp.exp(m_new - m_fin)
        p_new = jnp.exp(s_new - m_fin)
        o_ref[...] = (a2 * acc + p_new * vn_ref[...]) / (a2 * l_new + p_new)


def _fox_decode(page_table, fqb, fk_new, fv_new, lf_new, cache_k, cache_v, cache_lf, layer):
    bsz, n_pages = page_table.shape
    page = cache_k.shape[2]
    g = _largest_divisor(n_pages, (8, 4, 2, 1))
    n_steps = n_pages // g
    incl = _fox_suffix(page_table, cache_lf, layer, _largest_divisor(n_pages, (16, 8, 4, 2, 1)))
    bias = jnp.concatenate([incl[:, 1:], jnp.zeros((bsz, 1, H_F), F32)], axis=1) + lf_new[:, None, :]
    bias = bias.reshape(bsz, n_steps, 1, g * page * H_F)

    def page_spec(gg):
        return pl.BlockSpec((None, None, page, H_F, DH_F), lambda b, s, pt: (layer, pt[b, s * g + gg], 0, 0, 0))

    per_b = pl.BlockSpec((None, H_F, DH_F), lambda b, s, pt: (b, 0, 0))
    grid_spec = pltpu.PrefetchScalarGridSpec(
        num_scalar_prefetch=1, grid=(bsz, n_steps),
        in_specs=[per_b] + [page_spec(gg) for gg in range(g)] * 2 +
                 [pl.BlockSpec((None, None, 1, g * page * H_F), lambda b, s, pt: (b, s, 0, 0)), per_b, per_b],
        out_specs=per_b,
        scratch_shapes=[pltpu.VMEM((H_F, LANES), F32), pltpu.VMEM((H_F, LANES), F32), pltpu.VMEM((H_F, DH_F), F32)])
    out = pl.pallas_call(
        functools.partial(_fox_decode_kernel, g=g), grid_spec=grid_spec,
        out_shape=jax.ShapeDtypeStruct((bsz, H_F, DH_F), F32),
        compiler_params=_cparams("arbitrary", "arbitrary"), name="fox_decode",
    )(page_table, fqb.reshape(bsz, H_F, DH_F), *([cache_k] * g), *([cache_v] * g), bias,
      fk_new.reshape(bsz, H_F, DH_F), fv_new.reshape(bsz, H_F, DH_F))
    return out.reshape(bsz, H_F * DH_F)


def _mlstm_prompt_kernel(q_ref, k_ref, v_ref, cols_ref, ccols_ref, rows_ref, h_ref, cst_ref, mst_ref,
                         c_scr, m_scr, *, n_chunks, ln):
    i = pl.program_id(0)

    @pl.when(i == 0)
    def _():
        c_scr[...] = jnp.zeros_like(c_scr)
        m_scr[...] = jnp.zeros_like(m_scr)

    lane = lax.broadcasted_iota(I32, (ln, LANES), 1)
    one_col = jnp.where(lane == 0, 1.0, 0.0).astype(F32)
    tril = lax.broadcasted_iota(I32, (ln, ln), 1) <= lax.broadcasted_iota(I32, (ln, ln), 0)
    srow = lax.broadcasted_iota(I32, (LANES, 2 * LANES), 0)

    def chunk(c, _):
        t0 = pl.multiple_of(c * ln, ln)
        cols = cols_ref[pl.ds(t0, ln), :]
        ccols = ccols_ref[pl.ds(t0, ln), :]
        rows = rows_ref[c]
        for p in range(H_M // 2):
            q2 = q_ref[pl.ds(t0, ln), p * LANES:(p + 1) * LANES]
            k2 = k_ref[pl.ds(t0, ln), p * LANES:(p + 1) * LANES].astype(BF16)
            c_pair = c_scr[p]
            c_pair_b = c_pair.astype(BF16)
            new_pair = c_pair
            for hh in range(2):
                h = 2 * p + hh
                own = (lane < DK_M) if hh == 0 else (lane >= DK_M)
                qm = jnp.where(own, q2, 0.0).astype(BF16)
                v = v_ref[pl.ds(t0, ln), h * DV_M:(h + 1) * DV_M]
                v_aug = jnp.concatenate([v, one_col], axis=1)
                ig_col = cols[:, 8 + h:9 + h]
                bc_col = ccols[:, 12 + h:13 + h]
                ig_row = rows[8 + h:9 + h, :]
                bc_row = rows[28 + h:29 + h, :]
                m0 = m_scr[h][0:1, 0:1]
                d = jnp.where(tril, bc_col - bc_row + ig_row, -jnp.inf)
                inter = bc_col + m0
                m_row = jnp.maximum(jnp.max(d, axis=1, keepdims=True), inter)
                s_in = jnp.exp(inter - m_row)
                sq = _dot_nt(qm, k2) * jnp.exp(d - m_row)
                num_aug = _dot(sq.astype(BF16), v_aug.astype(BF16)) + s_in * _dot(qm, c_pair_b)
                den = num_aug[:, DV_M:DV_M + 1]
                hv = num_aug[:, :DV_M] / jnp.maximum(jnp.abs(den), jnp.exp(-m_row))
                h_ref[pl.ds(t0, ln), h * DV_M:(h + 1) * DV_M] = hv
                b_last = bc_col[ln - 1:ln, :]
                dec = b_last - bc_col + ig_col
                m_new = jnp.maximum(b_last + m0, jnp.max(dec, axis=0, keepdims=True))
                ws = jnp.exp(dec - m_new)
                s0 = jnp.exp(b_last + m0 - m_new)
                upd = s0 * c_pair + _dot_tn(k2, (ws * v_aug).astype(BF16))
                rows_own = (srow < DK_M) if hh == 0 else (srow >= DK_M)
                new_pair = jnp.where(rows_own, upd, new_pair)
                m_scr[h] = jnp.broadcast_to(m_new, (8, LANES))
            c_scr[p] = new_pair
        return 0

    lax.fori_loop(0, n_chunks, chunk, 0)
    cst_ref[...] = c_scr[...]
    mst_ref[...] = m_scr[...]


def _mlstm_prompt(mq, mk, mv, cols, ccols, rows, g):
    s = mq.shape[0]
    ln = M_CHUNK
    tb = g * ln
    rows_ch = rows.reshape(32, s // ln, ln).transpose(1, 0, 2)
    rowb = lambda n: pl.BlockSpec((tb, n), lambda i: (i, 0))
    h, cst, mst = pl.pallas_call(
        functools.partial(_mlstm_prompt_kernel, n_chunks=g, ln=ln), grid=(s // tb,),
        in_specs=[rowb(256), rowb(256), rowb(512), rowb(LANES), rowb(LANES),
                  pl.BlockSpec((g, 32, ln), lambda i: (i, 0, 0))],
        out_specs=[rowb(512), pl.BlockSpec((2, LANES, 2 * LANES), lambda i: (0, 0, 0)),
                   pl.BlockSpec((H_M, 8, LANES), lambda i: (0, 0, 0))],
        out_shape=[jax.ShapeDtypeStruct((s, 512), F32), jax.ShapeDtypeStruct((2, LANES, 2 * LANES), F32),
                   jax.ShapeDtypeStruct((H_M, 8, LANES), F32)],
        scratch_shapes=[pltpu.VMEM((2, LANES, 2 * LANES), F32), pltpu.VMEM((H_M, 8, LANES), F32)],
        compiler_params=_cparams("arbitrary"), name="mlstm_prompt",
    )(mq, mk, mv, cols, ccols, rows_ch)
    c_new = cst[:, :, :DV_M].reshape(H_M, DK_M, DV_M)
    n_new = cst[:, :, DV_M].reshape(H_M, DK_M)
    m_new = mst[:, 0, 0]
    return h, c_new, n_new, m_new


def _mlstm_step_kernel(q_ref, k_ref, v_ref, cols_ref, c_ref, n_ref, m_ref, h_ref, cn_ref, nn_ref, mn_ref):
    cols = cols_ref[...]
    lane = lax.broadcasted_iota(I32, (1, LANES), 1)
    lane8 = lax.broadcasted_iota(I32, (8, LANES), 1)
    row8 = lax.broadcasted_iota(I32, (8, LANES), 0)
    srow = lax.broadcasted_iota(I32, (LANES, 1), 0)
    m_out = jnp.zeros((1, LANES), F32)
    for p in range(H_M // 2):
        q2 = q_ref[:, p * LANES:(p + 1) * LANES]
        k2 = k_ref[:, p * LANES:(p + 1) * LANES]
        q2r = q2.astype(BF16).astype(F32)
        k2r = k2.astype(BF16).astype(F32)
        n_pair = n_ref[:, p * LANES:(p + 1) * LANES]
        c_pair = c_ref[p * LANES:(p + 1) * LANES, :]
        c_pair_b = c_pair.astype(BF16)
        upd = jnp.zeros((LANES, LANES), F32)
        s0_rows = jnp.zeros((LANES, 1), F32)
        n_new = jnp.zeros((1, LANES), F32)
        for hh in range(2):
            h = 2 * p + hh
            own = (lane < DK_M) if hh == 0 else (lane >= DK_M)
            own8 = (lane8 < DK_M) if hh == 0 else (lane8 >= DK_M)
            qf = jnp.where(own, q2r, 0.0)
            v = v_ref[:, h * DV_M:(h + 1) * DV_M]
            ig = cols[:, 8 + h:9 + h]
            lf = cols[:, 12 + h:13 + h]
            m0 = m_ref[:, h:h + 1]
            inter = lf + m0
            m_row = jnp.maximum(ig, inter)
            s_in = jnp.exp(inter - m_row)
            wqk = jnp.sum(qf * k2r, axis=1, keepdims=True) * jnp.exp(ig - m_row)
            q_c = _dot(jnp.broadcast_to(qf, (8, LANES)).astype(BF16), c_pair_b)[0:1, :]
            q_n = jnp.sum(qf * n_pair.astype(BF16).astype(F32), axis=1, keepdims=True)
            num = wqk * v.astype(BF16).astype(F32) + s_in * q_c
            den = wqk + s_in * q_n
            h_ref[:, h * DV_M:(h + 1) * DV_M] = num / jnp.maximum(jnp.abs(den), jnp.exp(-m_row))
            ws = jnp.exp(ig - m_row)
            km8 = jnp.where(row8 == 0, jnp.where(own8, jnp.broadcast_to(k2r, (8, LANES)), 0.0), 0.0).astype(BF16)
            wv8 = jnp.where(row8 == 0, jnp.broadcast_to(ws * v, (8, LANES)), 0.0).astype(BF16)
            upd = upd + _dot_tn(km8, wv8)
            s0_rows = jnp.where((srow < DK_M) if hh == 0 else (srow >= DK_M), s_in, s0_rows)
            n_new = jnp.where(own, s_in * n_pair + ws * k2, n_new)
            m_out = jnp.where(lane == h, m_row, m_out)
        cn_ref[p * LANES:(p + 1) * LANES, :] = s0_rows * c_pair + upd
        nn_ref[:, p * LANES:(p + 1) * LANES] = n_new
    mn_ref[...] = m_out


def _mlstm_step(mq, mk, mv, cols, c0, n0, m0):
    bsz = mq.shape[0]
    b3 = lambda r, n: pl.BlockSpec((None, r, n), lambda i: (i, 0, 0))
    h, cn, nn, mn = pl.pallas_call(
        _mlstm_step_kernel, grid=(bsz,),
        in_specs=[b3(1, 256), b3(1, 256), b3(1, 512), b3(1, LANES), b3(H_M * DK_M, DV_M), b3(1, 256), b3(1, H_M)],
        out_specs=[b3(1, 512), b3(H_M * DK_M, DV_M), b3(1, 256), b3(1, LANES)],
        out_shape=[jax.ShapeDtypeStruct((bsz, 1, 512), F32), jax.ShapeDtypeStruct((bsz, H_M * DK_M, DV_M), F32),
                   jax.ShapeDtypeStruct((bsz, 1, 256), F32), jax.ShapeDtypeStruct((bsz, 1, LANES), F32)],
        compiler_params=_cparams("arbitrary"), name="mlstm_step",
    )(mq[:, None, :], mk[:, None, :], mv[:, None, :], cols[:, None, :],
      c0.reshape(bsz, H_M * DK_M, DV_M), n0.reshape(bsz, 1, H_M * DK_M), m0.reshape(bsz, 1, H_M))
    return (h[:, 0, :], cn.reshape(bsz, H_M, DK_M, DV_M), nn.reshape(bsz, H_M, DK_M), mn[:, 0, :H_M])


def _merge_kernel(x_ref, cy_ref, fy_ref, mh_ref, mo_ref, wg_ref, bg_ref, gm_ref, wb_ref, wo_ref, g_ref, b_ref, o_ref):
    x = x_ref[...]
    gates = jax.nn.sigmoid(_dot(x.astype(BF16), wg_ref[...]) + bg_ref[...])
    segs = []
    for h in range(H_M):
        seg = mh_ref[:, h * DV_M:(h + 1) * DV_M]
        segs.append(seg * lax.rsqrt(jnp.mean(seg * seg, axis=-1, keepdims=True) + LN_EPS))
    my = (jnp.concatenate(segs, axis=1) * gm_ref[...] * mo_ref[...]).astype(BF16)
    mix = gates[:, :D_MODEL] * _dot(cy_ref[...], wb_ref[0])
    mix = mix + gates[:, D_MODEL:2 * D_MODEL] * _dot(fy_ref[...], wb_ref[1])
    mix = mix + gates[:, 2 * D_MODEL:] * _dot(my, wb_ref[2])
    y = _dot(mix.astype(BF16), wo_ref[...])
    o_ref[...] = _layer_norm(ALPHA * x + y, g_ref[...], b_ref[...])


def _merge(x, conv_y, fox_y, m_h, m_o, w, tm):
    r = x.shape[0]
    rowb = lambda n: pl.BlockSpec((tm, n), lambda i: (i, 0))
    full = lambda a: pl.BlockSpec(a.shape, lambda i: (0,) * a.ndim)
    consts = [w['wg'], w['bg'], w['g_mnorm'], w['w_branch'], w['w_mix_out'], w['g_ln1'], w['b_ln1']]
    return pl.pallas_call(
        _merge_kernel, grid=(r // tm,),
        in_specs=[rowb(D_MODEL), rowb(512), rowb(512), rowb(512), rowb(512)] + [full(a) for a in consts],
        out_specs=rowb(D_MODEL), out_shape=jax.ShapeDtypeStruct((r, D_MODEL), F32),
        compiler_params=_cparams("arbitrary"), name="merge",
    )(x, conv_y, fox_y, m_h, m_o, *consts)


def _memkv_kernel(mem_ref, wk_ref, wv_ref, k_ref, v_ref, kb_ref, vb_ref):
    mb = mem_ref[...].astype(BF16)
    k = _dot(mb, wk_ref[...])
    v = _dot(mb, wv_ref[...])
    k_ref[...] = k
    v_ref[...] = v
    kb_ref[...] = k.astype(BF16)
    vb_ref[...] = v.astype(BF16)


def _memkv(mem, wk, wv):
    n = mem.shape[0]
    full = lambda a: pl.BlockSpec(a.shape, lambda i: (0,) * a.ndim)
    sd = lambda dt: jax.ShapeDtypeStruct((n, D_MODEL), dt)
    return pl.pallas_call(
        _memkv_kernel, grid=(1,),
        in_specs=[full(mem), full(wk), full(wv)],
        out_specs=[pl.BlockSpec((n, D_MODEL), lambda i: (0, 0))] * 4,
        out_shape=[sd(F32), sd(F32), sd(BF16), sd(BF16)],
        compiler_params=_cparams("arbitrary"), name="memkv",
    )(mem, wk, wv)


def _xattn_prompt_kernel(x_ref, kb_ref, vb_ref, wq_ref, wo_ref, g_ref, b_ref, o_ref):
    x = x_ref[...]
    q = (_dot(x.astype(BF16), wq_ref[...]) * DH_X ** -0.5).astype(BF16)
    outs = []
    for h in range(H_X):
        sl = slice(h * DH_X, (h + 1) * DH_X)
        s = _dot_nt(q[:, sl], kb_ref[:, sl])
        s = s - jnp.max(s, axis=1, keepdims=True)
        p = jnp.exp(s)
        o = _dot(p.astype(BF16), vb_ref[:, sl]) / jnp.sum(p, axis=1, keepdims=True)
        outs.append(o.astype(BF16))
    y = _dot(jnp.concatenate(outs, axis=1), wo_ref[...])
    o_ref[...] = _layer_norm(ALPHA * x + y, g_ref[...], b_ref[...])


def _xattn_prompt(x, kb, vb, w, tm):
    r = x.shape[0]
    rowb = pl.BlockSpec((tm, D_MODEL), lambda i: (i, 0))
    full = lambda a: pl.BlockSpec(a.shape, lambda i: (0,) * a.ndim)
    consts = [kb, vb, w['w_xq'], w['w_xo'], w['g_ln2'], w['b_ln2']]
    return pl.pallas_call(
        _xattn_prompt_kernel, grid=(r // tm,),
        in_specs=[rowb] + [full(a) for a in consts],
        out_specs=rowb, out_shape=jax.ShapeDtypeStruct((r, D_MODEL), F32),
        compiler_params=_cparams("arbitrary"), name="xattn_prompt",
    )(x, *consts)


def _linear_kernel(x_ref, w_ref, o_ref, *, scale):
    o_ref[...] = (_dot(x_ref[...].astype(BF16), w_ref[...]) * scale).astype(o_ref.dtype)


def _linear(x, w, scale, out_dtype):
    r, n = x.shape[0], w.shape[1]
    full = lambda a: pl.BlockSpec(a.shape, lambda i: (0,) * a.ndim)
    return pl.pallas_call(
        functools.partial(_linear_kernel, scale=scale), grid=(1,),
        in_specs=[full(x), full(w)], out_specs=pl.BlockSpec((r, n), lambda i: (0, 0)),
        out_shape=jax.ShapeDtypeStruct((r, n), out_dtype),
        compiler_params=_cparams("arbitrary"), name="linear",
    )(x, w)


def _linear_ln_kernel(x_ref, a_ref, w_ref, g_ref, b_ref, o_ref):
    y = _dot(a_ref[...].astype(BF16), w_ref[...])
    o_ref[...] = _layer_norm(ALPHA * x_ref[...] + y, g_ref[...], b_ref[...])


def _linear_ln(x, a, w, g, b):
    full = lambda t: pl.BlockSpec(t.shape, lambda i: (0,) * t.ndim)
    return pl.pallas_call(
        _linear_ln_kernel, grid=(1,),
        in_specs=[full(x), full(a), full(w), full(g), full(b)],
        out_specs=pl.BlockSpec(x.shape, lambda i: (0, 0)),
        out_shape=jax.ShapeDtypeStruct(x.shape, F32),
        compiler_params=_cparams("arbitrary"), name="linear_ln",
    )(x, a, w, g, b)


def _xattn_decode_kernel(q_ref, k_ref, v_ref, o_ref):
    w = q_ref.shape[-1]
    row = lax.broadcasted_iota(I32, (8, w), 0)
    lane = lax.broadcasted_iota(I32, (8, w), 1)
    own = (lane // DH_X) == row
    qrows = jnp.where(own, jnp.broadcast_to(q_ref[...].astype(F32), (8, w)), 0.0).astype(BF16)
    s = _dot_nt(qrows, k_ref[...].astype(BF16))
    s = s - jnp.max(s, axis=1, keepdims=True)
    p = jnp.exp(s)
    o = _dot(p.astype(BF16), v_ref[...].astype(BF16)) / jnp.sum(p, axis=1, keepdims=True)
    o_ref[...] = jnp.sum(jnp.where(own, o, 0.0), axis=0, keepdims=True)


def _xattn_decode(q, mem_k, mem_v):
    bsz, n_mem = mem_k.shape[0], mem_k.shape[1]
    b3 = lambda r: pl.BlockSpec((None, r, D_MODEL), lambda i: (i, 0, 0))
    out = pl.pallas_call(
        _xattn_decode_kernel, grid=(bsz,),
        in_specs=[b3(1), b3(n_mem), b3(n_mem)], out_specs=b3(1),
        out_shape=jax.ShapeDtypeStruct((bsz, 1, D_MODEL), F32),
        compiler_params=_cparams("arbitrary"), name="xattn_decode",
    )(q[:, None, :], mem_k.reshape(bsz, n_mem, D_MODEL), mem_v.reshape(bsz, n_mem, D_MODEL))
    return out[:, 0, :]


def _router_kernel(x_ref, w_ref, b_ref, lst_ref, ti_ref, tg_ref, cnt_ref, carry_ref):
    i = pl.program_id(0)

    @pl.when(i == 0)
    def _():
        carry_ref[...] = jnp.zeros_like(carry_ref)

    x1, x2, x3 = _split3(x_ref[...])
    w1, w2, w3 = w_ref[0], w_ref[1], w_ref[2]
    logits = (_dot(x1, w1) + (_dot(x1, w2) + _dot(x2, w1)) + (_dot(x1, w3) + _dot(x2, w2) + _dot(x3, w1))) + b_ref[...]
    tm = logits.shape[0]
    lane = lax.broadcasted_iota(I32, (tm, LANES), 1)
    lane_f = lane.astype(F32)
    vals = logits
    tops, idxs, sels = [], [], []
    for _ in range(TOP_K):
        mx = jnp.max(vals, axis=1, keepdims=True)
        idx = jnp.min(jnp.where(vals == mx, lane_f, float(LANES)), axis=1, keepdims=True)
        sel = lane_f == idx
        tops.append(mx)
        idxs.append(idx)
        sels.append(sel)
        vals = jnp.where(sel, -jnp.inf, vals)
    exps = [jnp.exp(t - tops[0]) for t in tops]
    den = exps[0] + exps[1] + exps[2] + exps[3]
    cnt = jnp.zeros((tm, LANES), F32)
    for sel in sels:
        cnt = cnt + jnp.where(sel, 1.0, 0.0)
    excl = _dot(lst_ref[...], cnt.astype(BF16)) + carry_ref[0:1, :]
    ti = jnp.zeros((tm, LANES), F32)
    tg = jnp.zeros((tm, LANES), F32)
    for k in range(TOP_K):
        rank = jnp.sum(jnp.where(sels[k], excl, 0.0), axis=1, keepdims=True)
        ti = jnp.where(lane == k, idxs[k], ti)
        ti = jnp.where(lane == TOP_K + k, rank, ti)
        tg = jnp.where(lane == k, exps[k] / den, tg)
    ti_ref[...] = ti.astype(I32)
    tg_ref[...] = tg
    carry_ref[...] = carry_ref[...] + jnp.sum(cnt, axis=0, keepdims=True)
    cnt_ref[...] = carry_ref[...]


def _router(x, w3, b, tm):
    r = x.shape[0]
    t = np.arange(tm)
    lst = jnp.asarray(t[:, None] > t[None, :], BF16)
    full = lambda a: pl.BlockSpec(a.shape, lambda i: (0,) * a.ndim)
    rowb = lambda n: pl.BlockSpec((tm, n), lambda i: (i, 0))
    return pl.pallas_call(
        _router_kernel, grid=(r // tm,),
        in_specs=[rowb(D_MODEL), full(w3), full(b), full(lst)],
        out_specs=[rowb(LANES), rowb(LANES), pl.BlockSpec((8, LANES), lambda i: (0, 0))],
        out_shape=[jax.ShapeDtypeStruct((r, LANES), I32), jax.ShapeDtypeStruct((r, LANES), F32),
                   jax.ShapeDtypeStruct((8, LANES), F32)],
        scratch_shapes=[pltpu.VMEM((8, LANES), F32)],
        compiler_params=_cparams("arbitrary"), name="router",
    )(x, w3, b, lst)


def _moe_dispatch_kernel(idx_hbm, x_ref, xs_in, xs_hbm, idx_smem, sem_idx, sem):
    del xs_in
    i = pl.program_id(0)
    tm = x_ref.shape[0]
    idx_cp = pltpu.make_async_copy(idx_hbm.at[i], idx_smem, sem_idx)
    idx_cp.start()
    idx_cp.wait()

    def send(r, _):
        for k in range(TOP_K):
            pltpu.make_async_copy(x_ref.at[pl.ds(r, 1)], xs_hbm.at[pl.ds(idx_smem[r * TOP_K + k], 1)], sem).start()
        return 0

    lax.fori_loop(0, tm, send, 0)
    for k in range(TOP_K):
        pltpu.make_async_copy(x_ref, xs_hbm.at[pl.ds(0, tm)], sem).wait()


def _moe_dispatch(idx, x, rows, tm):
    n = x.shape[0]
    return pl.pallas_call(
        _moe_dispatch_kernel, grid=(n // tm,),
        in_specs=[pl.BlockSpec(memory_space=pl.ANY), pl.BlockSpec((tm, D_MODEL), lambda i: (i, 0)),
                  pl.BlockSpec(memory_space=pl.ANY)],
        out_specs=pl.BlockSpec(memory_space=pl.ANY),
        out_shape=jax.ShapeDtypeStruct((rows, D_MODEL), F32),
        scratch_shapes=[pltpu.SMEM((tm * TOP_K,), I32), pltpu.SemaphoreType.DMA(()), pltpu.SemaphoreType.DMA(())],
        input_output_aliases={2: 0},
        compiler_params=_cparams("arbitrary"), name="moe_dispatch",
    )(idx, x, jnp.zeros((rows, D_MODEL), F32))


def _moe_ffn_kernel(be_ref, nv_ref, xs_ref, wgu_ref, bgu_ref, wd_ref, bd_ref, y_ref, wgu_b, wd_b):
    i = pl.program_id(0)

    @pl.when(i < nv_ref[0])
    def _():
        e = be_ref[i]
        e_prev = be_ref[jnp.maximum(i - 1, 0)]

        @pl.when((i == 0) | (e != e_prev))
        def _():
            wgu_b[...] = wgu_ref[...].astype(BF16)
            wd_b[...] = wd_ref[...].astype(BF16)

        gu = _dot(xs_ref[...].astype(BF16), wgu_b[...]) + bgu_ref[...]
        g = jnp.minimum(gu[:, :D_EXPERT], SWIGLU_LIMIT)
        u = jnp.clip(gu[:, D_EXPERT:], -SWIGLU_LIMIT, SWIGLU_LIMIT)
        act = (u + 1.0) * g * jax.nn.sigmoid(SWIGLU_ALPHA * g)
        y_ref[...] = _dot(act.astype(BF16), wd_b[...]) + bd_ref[...]

    @pl.when(i >= nv_ref[0])
    def _():
        y_ref[...] = jnp.zeros_like(y_ref)


def _moe_ffn(blk_e, n_valid, xs, w, l, tb):
    rows = xs.shape[0]
    wsel = lambda i, be, nv: (l, be[i], 0, 0)
    grid_spec = pltpu.PrefetchScalarGridSpec(
        num_scalar_prefetch=2, grid=(rows // tb,),
        in_specs=[pl.BlockSpec((tb, D_MODEL), lambda i, be, nv: (jnp.minimum(i, nv[0] - 1), 0)),
                  pl.BlockSpec((None, None, D_MODEL, 2 * D_EXPERT), wsel),
                  pl.BlockSpec((None, None, 1, 2 * D_EXPERT), wsel),
                  pl.BlockSpec((None, None, D_EXPERT, D_MODEL), wsel),
                  pl.BlockSpec((None, None, 1, D_MODEL), wsel)],
        out_specs=pl.BlockSpec((tb, D_MODEL), lambda i, be, nv: (i, 0)),
        scratch_shapes=[pltpu.VMEM((D_MODEL, 2 * D_EXPERT), BF16), pltpu.VMEM((D_EXPERT, D_MODEL), BF16)])
    return pl.pallas_call(
        _moe_ffn_kernel, grid_spec=grid_spec,
        out_shape=jax.ShapeDtypeStruct((rows, D_MODEL), F32),
        compiler_params=_cparams("arbitrary"), name="moe_ffn",
    )(blk_e, n_valid, xs, w['w_gate_up'], w['b_gate_up'][:, :, None, :], w['w_down'], w['b_down'][:, :, None, :])


def _moe_combine_kernel(idx_hbm, x_ref, tg_ref, ys_hbm, g_ref, b_ref, o_ref, idx_smem, ybuf, sem_idx, sem):
    i = pl.program_id(0)
    tm = x_ref.shape[0]
    idx_cp = pltpu.make_async_copy(idx_hbm.at[i], idx_smem, sem_idx)
    idx_cp.start()
    idx_cp.wait()

    def fetch(r, _):
        for k in range(TOP_K):
            pltpu.make_async_copy(ys_hbm.at[pl.ds(idx_smem[r * TOP_K + k], 1)], ybuf.at[k, pl.ds(r, 1)], sem).start()
        return 0

    lax.fori_loop(0, tm, fetch, 0)
    for k in range(TOP_K):
        pltpu.make_async_copy(ys_hbm.at[pl.ds(0, tm)], ybuf.at[k], sem).wait()
    tg = tg_ref[...]
    ff = tg[:, 0:1] * ybuf[0]
    for k in range(1, TOP_K):
        ff = ff + tg[:, k:k + 1] * ybuf[k]
    o_ref[...] = _layer_norm(ALPHA * x_ref[...] + ff, g_ref[...], b_ref[...])


def _moe_combine(idx, x, tg, ys, g, b, tm):
    n = x.shape[0]
    rowb = lambda wd: pl.BlockSpec((tm, wd), lambda i: (i, 0))
    full = lambda t: pl.BlockSpec(t.shape, lambda i: (0,) * t.ndim)
    return pl.pallas_call(
        _moe_combine_kernel, grid=(n // tm,),
        in_specs=[pl.BlockSpec(memory_space=pl.ANY), rowb(D_MODEL), rowb(LANES), pl.BlockSpec(memory_space=pl.ANY),
                  full(g), full(b)],
        out_specs=rowb(D_MODEL), out_shape=jax.ShapeDtypeStruct((n, D_MODEL), F32),
        scratch_shapes=[pltpu.SMEM((tm * TOP_K,), I32), pltpu.VMEM((TOP_K, tm, D_MODEL), F32),
                        pltpu.SemaphoreType.DMA(()), pltpu.SemaphoreType.DMA(())],
        compiler_params=_cparams("arbitrary"), name="moe_combine",
    )(idx, x, tg, ys, g, b)


def _moe(x, w, l, tb, tm):
    n = x.shape[0]
    a = n * TOP_K
    ti, tg, cnt = _router(x, w['w_router3'], w['b_router'], tm)
    top_e = ti[:, :TOP_K]
    rank = ti[:, TOP_K:2 * TOP_K]
    counts = cnt[0, :N_EXPERTS].astype(I32)
    n_blk = -(-a // tb) + N_EXPERTS
    padded = (counts + tb - 1) // tb * tb
    pend = jnp.cumsum(padded)
    dest = (pend - padded)[top_e] + rank
    idx = dest.reshape(n // tm, tm * TOP_K)
    n_valid = (pend[-1:] // tb).astype(I32)
    blk_id = jnp.minimum(jnp.arange(n_blk, dtype=I32), n_valid - 1)
    blk_e = jnp.minimum(jnp.searchsorted(pend, blk_id * tb, side='right'), N_EXPERTS - 1).astype(I32)
    xs = _moe_dispatch(idx, x, n_blk * tb, tm)
    ys = _moe_ffn(blk_e, n_valid, xs, w, l, tb)
    return _moe_combine(idx, x, tg, ys, w['g_ln3'], w['b_ln3'], tm)


def _layer_weights(l, w_in, b_in, b_fox_f, b_mlstm_f, g_mlstm_norm, w_branch, w_mix_out, g_ln1, b_ln1,
                   w_xq, w_xk, w_xv, w_xo, g_ln2, b_ln2, w_router, b_router, w_gate_up, b_gate_up,
                   w_down, b_down, g_ln3, b_ln3):
    w = _split_w_in(w_in[l], b_in[l], b_fox_f[l], b_mlstm_f[l])
    wr = jnp.pad(w_router[l], ((0, 0), (0, LANES - N_EXPERTS)))
    r1 = wr.astype(BF16)
    r2 = (wr - r1.astype(F32)).astype(BF16)
    r3 = (wr - r1.astype(F32) - r2.astype(F32)).astype(BF16)
    w.update(
        g_mnorm=g_mlstm_norm[l][None], w_branch=w_branch[l].astype(BF16), w_mix_out=w_mix_out[l].astype(BF16),
        g_ln1=g_ln1[l][None], b_ln1=b_ln1[l][None],
        w_xq=w_xq[l].astype(BF16), w_xk=w_xk[l].astype(BF16), w_xv=w_xv[l].astype(BF16), w_xo=w_xo[l].astype(BF16),
        g_ln2=g_ln2[l][None], b_ln2=b_ln2[l][None],
        w_router3=jnp.stack([r1, r2, r3]),
        b_router=jnp.pad(b_router[l], (0, LANES - N_EXPERTS), constant_values=NEG_BIG)[None],
        w_gate_up=w_gate_up, b_gate_up=b_gate_up, w_down=w_down, b_down=b_down,
        g_ln3=g_ln3[l][None], b_ln3=b_ln3[l][None])
    return w


def kernel(x_prompt, x_sample, mem_prompt, cache_fox_k, cache_fox_v, cache_fox_lf, page_table, state_conv,
           state_mlstm_c, state_mlstm_n, state_mlstm_m, cache_mem_k, cache_mem_v, w_in, b_in, b_fox_f, b_mlstm_f,
           w_dw, b_dw, g_conv_ln, b_conv_ln, g_mlstm_norm, w_branch, w_mix_out, g_ln1, b_ln1, w_xq, w_xk, w_xv,
           w_xo, g_ln2, b_ln2, w_router, b_router, w_gate_up, b_gate_up, w_down, b_down, g_ln3, b_ln3):
    b_p, s, _ = x_prompt.shape
    b_s = x_sample.shape[0]
    assert b_p == 1 and x_sample.shape[1] == 1
    depth = w_in.shape[0]
    xp = x_prompt.reshape(s, D_MODEL)
    xs = x_sample.reshape(b_s, D_MODEL)
    mem = mem_prompt.reshape(-1, D_MODEL)
    tm = _row_block(s, 256)
    tq = _row_block(s, 256)
    tk = _row_block(s, 512)
    g_chunks = 8 if s % (8 * M_CHUNK) == 0 else 1
    tb_p = 256 if s * TOP_K >= 256 * N_EXPERTS else 8
    rows_p, rows_s = [], []
    for l in range(depth):
        w = _layer_weights(l, w_in, b_in, b_fox_f, b_mlstm_f, g_mlstm_norm, w_branch, w_mix_out, g_ln1, b_ln1,
                           w_xq, w_xk, w_xv, w_xo, g_ln2, b_ln2, w_router, b_router, w_gate_up, b_gate_up,
                           w_down, b_down, g_ln3, b_ln3)
        pr = _inproj(xp, w, tm)
        conv_y = _conv_prompt(pr['u'], w_dw[l], b_dw[l], g_conv_ln[l], b_conv_ln[l], tm)
        fox_y = _fox_prompt(pr['fqb'], pr['fkb'], pr['fvb'], pr['rows'][16:24], tq, tk)
        m_h, m_c, m_n, m_m = _mlstm_prompt(pr['mq'], pr['mk'], pr['mv'], pr['cols'], pr['ccols'], pr['rows'], g_chunks)
        xp = _merge(xp, conv_y, fox_y, m_h, pr['mo'], w, tm)
        mem_k, mem_v, mem_kb, mem_vb = _memkv(mem, w['w_xk'], w['w_xv'])
        xp = _xattn_prompt(xp, mem_kb, mem_vb, w, tm)
        xp = _moe(xp, w, l, tb_p, tm)
        rows_p.append((pr['fk'].reshape(1, s, H_F, DH_F), pr['fv'].reshape(1, s, H_F, DH_F),
                       pr['cols'][:, :H_F].reshape(1, s, H_F), pr['u'][s - (K_CONV - 1):][None],
                       m_c[None], m_n[None], m_m[None],
                       mem_k.reshape(1, -1, H_X, DH_X), mem_v.reshape(1, -1, H_X, DH_X)))
        sr = _inproj(xs, w, b_s)
        conv_y, conv_st = _conv_step(state_conv[l], sr['u'], w_dw[l], b_dw[l], g_conv_ln[l], b_conv_ln[l])
        lf_new = sr['cols'][:, :H_F]
        fox_y = _fox_decode(page_table, sr['fqb'], sr['fk'], sr['fv'], lf_new, cache_fox_k, cache_fox_v,
                            cache_fox_lf, l)
        m_h, m_c, m_n, m_m = _mlstm_step(sr['mq'], sr['mk'], sr['mv'], sr['cols'], state_mlstm_c[l],
                                         state_mlstm_n[l], state_mlstm_m[l])
        xs = _merge(xs, conv_y, fox_y.astype(BF16), m_h, sr['mo'], w, b_s)
        q = _linear(xs, w['w_xq'], DH_X ** -0.5, BF16)
        att = _xattn_decode(q, cache_mem_k[l], cache_mem_v[l])
        xs = _linear_ln(xs, att, w['w_xo'], w['g_ln2'], w['b_ln2'])
        xs = _moe(xs, w, l, 8, b_s)
        rows_s.append((sr['fk'].reshape(b_s, 1, H_F, DH_F), sr['fv'].reshape(b_s, 1, H_F, DH_F),
                       lf_new.reshape(b_s, 1, H_F), conv_st, m_c, m_n, m_m))
    outs_p = [jnp.stack(a) for a in zip(*rows_p)]
    outs_s = [jnp.stack(a) for a in zip(*rows_s)]
    return (xp.reshape(1, s, D_MODEL), xs.reshape(b_s, 1, D_MODEL), *outs_p, *outs_s)
```

```python
import functools

import numpy as np
import jax
import jax.numpy as jnp
from jax import lax
from jax.experimental import pallas as pl
from jax.experimental.pallas import tpu as pltpu

F32 = jnp.float32
BF16 = jnp.bfloat16
I32 = jnp.int32

D_MODEL = 1024
W_BR = 512
K_CONV = 31
H_F = 8
DH_F = 64
H_M = 4
DK_M = 64
DV_M = 128
M_CHUNK = 64
I_CAP = 15.0
H_X = 4
DH_X = 256
N_EXPERTS = 32
TOP_K = 4
D_EXPERT = 1024
SWIGLU_ALPHA = 1.702
SWIGLU_LIMIT = 7.0
LN_EPS = 1e-5
DEPTH = 2
ALPHA = (2.0 * DEPTH) ** 0.25
NEG_BIG = -1e30
LANES = 128
VMEM_LIMIT = 56 * 1024 * 1024


def _cparams(*sem):
    return pltpu.CompilerParams(dimension_semantics=sem, vmem_limit_bytes=VMEM_LIMIT)


def _dot(a, b):
    return jnp.dot(a, b, preferred_element_type=F32)


def _dot_nt(a, b):
    return lax.dot_general(a, b, (((1,), (1,)), ((), ())), preferred_element_type=F32)


def _dot_tn(a, b):
    return lax.dot_general(a, b, (((0,), (0,)), ((), ())), preferred_element_type=F32)


def _split3(a):
    a1 = a.astype(BF16)
    r = a - a1.astype(F32)
    a2 = r.astype(BF16)
    r = r - a2.astype(F32)
    return a1, a2, r.astype(BF16)


def _log_sigmoid(x):
    return jnp.minimum(x, 0.0) - jnp.log1p(jnp.exp(-jnp.abs(x)))


def _layer_norm(x, g, b):
    mu = jnp.mean(x, axis=-1, keepdims=True)
    xc = x - mu
    var = jnp.mean(xc * xc, axis=-1, keepdims=True)
    return xc * lax.rsqrt(var + LN_EPS) * g + b


def _row_block(n, pref):
    return pref if n % pref == 0 else n


def _gate_act(z, idx):
    ls = _log_sigmoid(z)
    tg = I_CAP * jnp.tanh(z * (1.0 / I_CAP))
    return jnp.where(idx < 8, ls, jnp.where(idx < 12, tg, jnp.where(idx < 16, ls, 0.0)))


def _inproj_kernel(x_ref, wc_ref, bc_ref, wf_ref, bf_ref, wm_ref, bm_ref, ws_ref, bs_ref, wst_ref, bst_ref,
                   ufull_ref, ublk_ref, lblk_ref,
                   u_ref, fqb_ref, fk_ref, fv_ref, fkb_ref, fvb_ref, mq_ref, mk_ref, mv_ref, mo_ref,
                   cols_ref, ccols_ref, rows_ref, carry_ref):
    i = pl.program_id(0)

    @pl.when(i == 0)
    def _():
        carry_ref[...] = jnp.zeros_like(carry_ref)

    tm = x_ref.shape[0]
    xb = x_ref[...].astype(BF16)
    zc = _dot(xb, wc_ref[...]) + bc_ref[...]
    u_ref[...] = zc[:, :W_BR] * jax.nn.sigmoid(zc[:, W_BR:])

    zf = _dot(xb, wf_ref[...]) + bf_ref[...]
    fk = zf[:, W_BR:2 * W_BR]
    fv = zf[:, 2 * W_BR:]
    fqb_ref[...] = (zf[:, :W_BR] * DH_F ** -0.5).astype(BF16)
    fk_ref[...] = fk
    fv_ref[...] = fv
    fkb_ref[...] = fk.astype(BF16)
    fvb_ref[...] = fv.astype(BF16)

    zm = _dot(xb, wm_ref[...]) + bm_ref[...]
    mq_ref[...] = zm[:, :256]
    mk_ref[...] = zm[:, 256:512] * DK_M ** -0.5
    mv_ref[...] = zm[:, 512:1024]
    mo_ref[...] = jax.nn.sigmoid(zm[:, 1024:])

    zs = _dot(xb, ws_ref[...]) + bs_ref[...]
    lane = lax.broadcasted_iota(I32, zs.shape, 1)
    act_c = _gate_act(zs, lane)
    cols_ref[...] = act_c
    c1, c2, c3 = _split3(act_c)
    lb = lblk_ref[...]
    ccols_ref[...] = _dot(lb, c1) + _dot(lb, c2) + _dot(lb, c3)

    zt = _dot_nt(wst_ref[...], xb) + bst_ref[...]
    row = lax.broadcasted_iota(I32, zt.shape, 0)
    act_r = _gate_act(zt, row)
    r1, r2, r3 = _split3(act_r)
    uf = ufull_ref[...]
    ub = ublk_ref[...]
    cum_full = _dot(r1, uf) + _dot(r2, uf) + _dot(r3, uf) + carry_ref[:, 0:1]
    cum_blk = _dot(r1, ub) + _dot(r2, ub) + _dot(r3, ub)
    carry_ref[...] = jnp.broadcast_to(cum_full[:, tm - 1:tm], carry_ref.shape)
    rows_ref[0:16, :] = act_r
    rows_ref[16:32, :] = jnp.where(row < 8, cum_full, cum_blk)


def _tri_consts(tm, chunk):
    t = np.arange(tm)
    upper = (t[:, None] <= t[None, :])
    same = (t[:, None] // chunk) == (t[None, :] // chunk)
    ufull = jnp.asarray(upper, BF16)
    ublk = jnp.asarray(upper & same, BF16)
    lblk = jnp.asarray(upper.T & same, BF16)
    return ufull, ublk, lblk


def _split_w_in(w_in, b_in, b_fox_f, b_mlstm_f):
    o = np.cumsum([0, 512, 512, 512, 512, 512, 8, 256, 256, 512, 4, 4, 512, 3072])
    sl = lambda a, i, j: a[..., o[i]:o[j]]
    wc, bc = sl(w_in, 0, 2), sl(b_in, 0, 2)
    wf, bf = sl(w_in, 2, 5), sl(b_in, 2, 5)
    wm = jnp.concatenate([sl(w_in, 6, 9), sl(w_in, 11, 12)], axis=-1)
    bm = jnp.concatenate([sl(b_in, 6, 9), sl(b_in, 11, 12)], axis=-1)
    wsm = jnp.concatenate([sl(w_in, 5, 6), sl(w_in, 9, 11)], axis=-1)
    bsm = jnp.concatenate([sl(b_in, 5, 6) + b_fox_f, sl(b_in, 9, 10), sl(b_in, 10, 11) + b_mlstm_f], axis=-1)
    ws = jnp.pad(wsm, ((0, 0), (0, LANES - 16)))
    bs = jnp.pad(bsm, (0, LANES - 16))
    wg, bg = sl(w_in, 12, 13), sl(b_in, 12, 13)
    return dict(wc=wc.astype(BF16), bc=bc[None], wf=wf.astype(BF16), bf=bf[None], wm=wm.astype(BF16), bm=bm[None],
                ws=ws.astype(BF16), bs=bs[None], wst=wsm.T.astype(BF16), bst=bsm[:, None],
                wg=wg.astype(BF16), bg=bg[None])


def _inproj(x, w, tm):
    r = x.shape[0]
    chunk = min(M_CHUNK, tm)
    ufull, ublk, lblk = _tri_consts(tm, chunk)
    rowb = lambda n: pl.BlockSpec((tm, n), lambda i: (i, 0))
    full = lambda a: pl.BlockSpec(a.shape, lambda i: (0,) * a.ndim)
    consts = [w['wc'], w['bc'], w['wf'], w['bf'], w['wm'], w['bm'], w['ws'], w['bs'], w['wst'], w['bst'],
              ufull, ublk, lblk]
    out_shape = [jax.ShapeDtypeStruct((r, 512), F32), jax.ShapeDtypeStruct((r, 512), BF16),
                 jax.ShapeDtypeStruct((r, 512), F32), jax.ShapeDtypeStruct((r, 512), F32),
                 jax.ShapeDtypeStruct((r, 512), BF16), jax.ShapeDtypeStruct((r, 512), BF16),
                 jax.ShapeDtypeStruct((r, 256), F32), jax.ShapeDtypeStruct((r, 256), F32),
                 jax.ShapeDtypeStruct((r, 512), F32), jax.ShapeDtypeStruct((r, 512), F32),
                 jax.ShapeDtypeStruct((r, LANES), F32), jax.ShapeDtypeStruct((r, LANES), F32),
                 jax.ShapeDtypeStruct((32, r), F32)]
    out_specs = [rowb(512)] * 6 + [rowb(256)] * 2 + [rowb(512)] * 2 + [rowb(LANES)] * 2 + \
                [pl.BlockSpec((32, tm), lambda i: (0, i))]
    outs = pl.pallas_call(
        _inproj_kernel, grid=(r // tm,),
        in_specs=[rowb(D_MODEL)] + [full(a) for a in consts],
        out_specs=out_specs, out_shape=out_shape,
        scratch_shapes=[pltpu.VMEM((16, LANES), F32)],
        compiler_params=_cparams("arbitrary"), name="inproj",
    )(x, *consts)
    keys = ['u', 'fqb', 'fk', 'fv', 'fkb', 'fvb', 'mq', 'mk', 'mv', 'mo', 'cols', 'ccols', 'rows']
    return dict(zip(keys, outs))


HALO = 32


def _conv_kernel(u_ref, w_ref, b_ref, g_ref, bln_ref, y_ref, buf_ref):
    i = pl.program_id(0)
    tm = u_ref.shape[0]

    @pl.when(i == 0)
    def _():
        buf_ref[0:HALO, :] = jnp.zeros((HALO, W_BR), F32)

    buf_ref[HALO:HALO + tm, :] = u_ref[...]
    acc = jnp.zeros((tm, W_BR), F32) + b_ref[...]
    off = HALO - (K_CONV - 1)
    for k in range(K_CONV):
        acc = acc + w_ref[k:k + 1, :] * buf_ref[off + k:off + k + tm, :]
    yn = _layer_norm(acc, g_ref[...], bln_ref[...])
    y_ref[...] = (yn * jax.nn.sigmoid(yn)).astype(y_ref.dtype)
    buf_ref[0:HALO, :] = buf_ref[tm:tm + HALO, :]


def _conv_prompt(u, w_dw, b_dw, g, b, tm):
    s = u.shape[0]
    full = lambda a: pl.BlockSpec(a.shape, lambda i: (0,) * a.ndim)
    args = [w_dw, b_dw[None], g[None], b[None]]
    return pl.pallas_call(
        _conv_kernel, grid=(s // tm,),
        in_specs=[pl.BlockSpec((tm, W_BR), lambda i: (i, 0))] + [full(a) for a in args],
        out_specs=pl.BlockSpec((tm, W_BR), lambda i: (i, 0)),
        out_shape=jax.ShapeDtypeStruct((s, W_BR), BF16),
        scratch_shapes=[pltpu.VMEM((HALO + tm, W_BR), F32)],
        compiler_params=_cparams("arbitrary"), name="conv_prompt",
    )(u, *args)


def _conv_step_kernel(st_ref, u_ref, w_ref, b_ref, g_ref, bln_ref, y_ref, nst_ref):
    st = st_ref[0]
    un = u_ref[0]
    acc = b_ref[...] + w_ref[K_CONV - 1:K_CONV, :] * un
    acc = acc + jnp.sum(w_ref[0:K_CONV - 1, :] * st, axis=0, keepdims=True)
    yn = _layer_norm(acc, g_ref[...], bln_ref[...])
    y_ref[0] = (yn * jax.nn.sigmoid(yn)).astype(y_ref.dtype)
    nst_ref[0, 0:K_CONV - 2, :] = st[1:, :]
    nst_ref[0, K_CONV - 2:K_CONV - 1, :] = un


def _conv_step(state, u, w_dw, b_dw, g, b):
    bsz = u.shape[0]
    full = lambda a: pl.BlockSpec(a.shape, lambda i: (0,) * a.ndim)
    args = [w_dw, b_dw[None], g[None], b[None]]
    y, nst = pl.pallas_call(
        _conv_step_kernel, grid=(bsz,),
        in_specs=[pl.BlockSpec((1, K_CONV - 1, W_BR), lambda i: (i, 0, 0)),
                  pl.BlockSpec((1, 1, W_BR), lambda i: (i, 0, 0))] + [full(a) for a in args],
        out_specs=[pl.BlockSpec((1, 1, W_BR), lambda i: (i, 0, 0)),
                   pl.BlockSpec((1, K_CONV - 1, W_BR), lambda i: (i, 0, 0))],
        out_shape=[jax.ShapeDtypeStruct((bsz, 1, W_BR), BF16),
                   jax.ShapeDtypeStruct((bsz, K_CONV - 1, W_BR), F32)],
        compiler_params=_cparams("arbitrary"), name="conv_step",
    )(state, u[:, None, :], *args)
    return y[:, 0, :], nst


def _fox_prompt_kernel(q_ref, k_ref, v_ref, c_ref, o_ref, s_a, s_b, p_a, p_b, m_s, l_s, a_s, acc_s, *, tq, tk):
    qi = pl.program_id(1)
    q2 = q_ref[...].astype(F32)
    lane = lax.broadcasted_iota(I32, (tq, LANES), 1)
    q_start = pl.multiple_of(qi * tq, tq)
    n_full = (qi * tq) // tk
    qms = [jnp.where(lane < DH_F, q2, 0.0).astype(BF16), jnp.where(lane < DH_F, 0.0, q2).astype(BF16)]
    c0s = [c_ref[hh:hh + 1, pl.ds(q_start, LANES)][:, 0:1] for hh in range(2)]

    def scores(j, s_ref):
        k_start = pl.multiple_of(j * tk, tk)
        kb = k_ref[pl.ds(k_start, tk), :]
        for hh in range(2):
            s_ref[hh] = _dot_nt(qms[hh], kb) + (c0s[hh] - c_ref[hh:hh + 1, pl.ds(k_start, tk)])

    def values(j, p_ref):
        k_start = pl.multiple_of(j * tk, tk)
        vb = v_ref[pl.ds(k_start, tk), :]
        for hh in range(2):
            acc_s[hh] = a_s[hh] * acc_s[hh] + _dot(p_ref[hh], vb)

    def softmax(j, s_ref, p_ref, masked):
        for hh in range(2):
            s = s_ref[hh]
            if masked:
                rows = q_start + lax.broadcasted_iota(I32, (tq, tk), 0)
                cols = j * tk + lax.broadcasted_iota(I32, (tq, tk), 1)
                s = jnp.where(cols <= rows, s, NEG_BIG)
            m_old = m_s[hh]
            m_new = jnp.maximum(m_old, jnp.max(s, axis=1, keepdims=True))
            a = jnp.exp(m_old - m_new)
            p = jnp.exp(s - m_new)
            p_ref[hh] = p.astype(BF16)
            m_s[hh] = m_new
            l_s[hh] = a * l_s[hh] + jnp.sum(p, axis=1, keepdims=True)
            a_s[hh] = a

    def stage(j, s_cur, s_nxt, p_cur, p_prev, last):
        if not last:
            scores(j + 1, s_nxt)
        values(jnp.maximum(j - 1, 0), p_prev)
        softmax(j, s_cur, p_cur, last)
        if last:
            values(j, p_cur)

    def by_parity(j, last):
        even = lax.rem(j, 2) == 0
        pl.when(even)(functools.partial(stage, j, s_a, s_b, p_a, p_b, last))
        pl.when(jnp.logical_not(even))(functools.partial(stage, j, s_b, s_a, p_b, p_a, last))

    p_b[...] = jnp.zeros_like(p_b)
    m_s[...] = jnp.full_like(m_s, NEG_BIG)
    l_s[...] = jnp.zeros_like(l_s)
    a_s[...] = jnp.ones_like(a_s)
    acc_s[...] = jnp.zeros_like(acc_s)
    scores(0, s_a)

    def body(j, _):
        by_parity(j, False)
        return 0

    lax.fori_loop(0, n_full, body, 0)
    by_parity(n_full, True)
    o_ref[...] = jnp.where(lane < DH_F, acc_s[0] / l_s[0], acc_s[1] / l_s[1]).astype(o_ref.dtype)


def _fox_prompt(fqb, fkb, fvb, crow, tq, tk):
    s = fqb.shape[0]
    c3 = crow.reshape(H_F // 2, 2, s)
    return pl.pallas_call(
        functools.partial(_fox_prompt_kernel, tq=tq, tk=tk), grid=(H_F // 2, s // tq),
        in_specs=[pl.BlockSpec((tq, LANES), lambda p, i: (i, p)),
                  pl.BlockSpec((s, LANES), lambda p, i: (0, p)),
                  pl.BlockSpec((s, LANES), lambda p, i: (0, p)),
                  pl.BlockSpec((None, 2, s), lambda p, i: (p, 0, 0))],
        out_specs=pl.BlockSpec((tq, LANES), lambda p, i: (i, p)),
        out_shape=jax.ShapeDtypeStruct((s, W_BR), BF16),
        scratch_shapes=[pltpu.VMEM((2, tq, tk), F32), pltpu.VMEM((2, tq, tk), F32),
                        pltpu.VMEM((2, tq, tk), BF16), pltpu.VMEM((2, tq, tk), BF16),
                        pltpu.VMEM((2, tq, 1), F32), pltpu.VMEM((2, tq, 1), F32), pltpu.VMEM((2, tq, 1), F32),
                        pltpu.VMEM((2, tq, LANES), F32)],
        compiler_params=_cparams("arbitrary", "arbitrary"), name="fox_prompt",
    )(fqb, fkb, fvb, c3)


def _largest_divisor(n, options):
    return next(o for o in options if n % o == 0)


def _fox_suffix_kernel(pt_ref, *refs, g):
    lf_refs, lfn_ref, tri_ref, o_ref, carry_ref = refs[:g], refs[g], refs[g + 1], refs[g + 2], refs[g + 3]

    @pl.when(pl.program_id(1) == 0)
    def _():
        carry_ref[...] = jnp.zeros_like(carry_ref)

    tri = tri_ref[...]
    carry = carry_ref[:, 0:1] + lfn_ref[...]
    for gg in reversed(range(g)):
        x = lf_refs[gg][...]
        x1, x2, x3 = _split3(x)
        excl = _dot(x1, tri) + _dot(x2, tri) + _dot(x3, tri) + carry
        o_ref[gg] = excl
        carry = excl[:, 0:1] + x[:, 0:1]
    carry_ref[...] = jnp.broadcast_to(carry - lfn_ref[...], carry_ref.shape)


def _fox_suffix(page_table, lf_t, lf_new_cols, layer, g):
    bsz, n_pages = page_table.shape
    page = lf_t.shape[-1]
    n_steps = n_pages // g
    t = np.arange(page)
    tri = jnp.asarray(t[:, None] > t[None, :], BF16)

    def page_spec(gg):
        return pl.BlockSpec((None, None, H_F, page),
                            lambda b, s, pt: (layer, pt[b, (n_steps - 1 - s) * g + gg], 0, 0))

    grid_spec = pltpu.PrefetchScalarGridSpec(
        num_scalar_prefetch=1, grid=(bsz, n_steps),
        in_specs=[page_spec(gg) for gg in range(g)] +
                 [pl.BlockSpec((None, H_F, 1), lambda b, s, pt: (b, 0, 0)),
                  pl.BlockSpec((page, page), lambda b, s, pt: (0, 0))],
        out_specs=pl.BlockSpec((None, g, H_F, page), lambda b, s, pt: (b, n_steps - 1 - s, 0, 0)),
        scratch_shapes=[pltpu.VMEM((H_F, LANES), F32)])
    return pl.pallas_call(
        functools.partial(_fox_suffix_kernel, g=g), grid_spec=grid_spec,
        out_shape=jax.ShapeDtypeStruct((bsz, n_pages, H_F, page), F32),
        compiler_params=_cparams("arbitrary", "arbitrary"), name="fox_suffix",
    )(page_table, *([lf_t] * g), lf_new_cols, tri)


def _fox_decode_kernel(pt_ref, *refs, g):
    q_ref, k_refs, v_refs = refs[0], refs[1:1 + g], refs[1 + g:1 + 2 * g]
    bias_ref, kn_ref, vn_ref, o_ref, qb_ref, m_ref, l_ref, acc_ref = refs[1 + 2 * g:]
    step = pl.program_id(1)
    page = k_refs[0].shape[-1]

    @pl.when(step == 0)
    def _():
        qb_ref[...] = jnp.broadcast_to(q_ref[...], qb_ref.shape)
        m_ref[...] = jnp.full_like(m_ref, NEG_BIG)
        l_ref[...] = jnp.zeros_like(l_ref)
        acc_ref[...] = jnp.zeros_like(acc_ref)

    last = step == pl.num_programs(1) - 1
    for h in range(H_F):
        qb = qb_ref[h]
        rows = [jnp.sum(k_refs[gg][h] * qb, axis=0, keepdims=True) + bias_ref[gg, h:h + 1, :] for gg in range(g)]
        s = jnp.concatenate(rows, axis=0)
        m_old = m_ref[h][:, 0:1]
        m_new = jnp.maximum(m_old, jnp.max(jnp.max(s, axis=1, keepdims=True), axis=0, keepdims=True))
        a = jnp.exp(m_old - m_new)
        p = jnp.exp(s - m_new)
        l_new = a * l_ref[h][:, 0:1] + jnp.sum(jnp.sum(p, axis=1, keepdims=True), axis=0, keepdims=True)
        acc = a * acc_ref[h]
        for gg in range(g):
            acc = acc + v_refs[gg][h] * p[gg:gg + 1, :]
        m_ref[h] = jnp.broadcast_to(m_new, (1, page))
        l_ref[h] = jnp.broadcast_to(l_new, (1, page))
        acc_ref[h] = acc

        @pl.when(last)
        def _():
            s_new = jnp.sum(q_ref[h] * kn_ref[h], axis=0, keepdims=True)
            m_fin = jnp.maximum(m_new, s_new)
            a2 = jnp.exp(m_new - m_fin)
            p_new = jnp.exp(s_new - m_fin)
            num = a2 * jnp.sum(acc, axis=1, keepdims=True) + p_new * vn_ref[h]
            o_ref[h] = num / (a2 * l_new + p_new)


def _fox_decode(page_table, fqb, fk_new, fv_new, lf_new, cache_k, cache_v, cache_lf, layer):
    bsz, n_pages = page_table.shape
    page = cache_k.shape[2]
    g = _largest_divisor(n_pages, (8, 4, 2, 1))
    n_steps = n_pages // g
    k_t = jnp.transpose(cache_k, (0, 1, 3, 4, 2))
    v_t = jnp.transpose(cache_v, (0, 1, 3, 4, 2))
    lf_t = jnp.transpose(cache_lf, (0, 1, 3, 2))
    bias = _fox_suffix(page_table, lf_t, lf_new[:, :, None], layer, _largest_divisor(n_pages, (16, 8, 4, 2, 1)))
    col = lambda a: a.astype(F32).reshape(bsz, H_F, DH_F, 1)

    def page_spec(gg):
        return pl.BlockSpec((None, None, H_F, DH_F, page), lambda b, s, pt: (layer, pt[b, s * g + gg], 0, 0, 0))

    per_b = pl.BlockSpec((None, H_F, DH_F, 1), lambda b, s, pt: (b, 0, 0, 0))
    grid_spec = pltpu.PrefetchScalarGridSpec(
        num_scalar_prefetch=1, grid=(bsz, n_steps),
        in_specs=[per_b] + [page_spec(gg) for gg in range(g)] * 2 +
                 [pl.BlockSpec((None, g, H_F, page), lambda b, s, pt: (b, s, 0, 0)), per_b, per_b],
        out_specs=per_b,
        scratch_shapes=[pltpu.VMEM((H_F, DH_F, page), F32), pltpu.VMEM((H_F, 1, page), F32),
                        pltpu.VMEM((H_F, 1, page), F32), pltpu.VMEM((H_F, DH_F, page), F32)])
    out = pl.pallas_call(
        functools.partial(_fox_decode_kernel, g=g), grid_spec=grid_spec,
        out_shape=jax.ShapeDtypeStruct((bsz, H_F, DH_F, 1), F32),
        compiler_params=_cparams("arbitrary", "arbitrary"), name="fox_decode",
    )(page_table, col(fqb), *([k_t] * g), *([v_t] * g), bias, col(fk_new), col(fv_new))
    return out.reshape(bsz, H_F * DH_F)


def _mlstm_prompt_kernel(q_ref, k_ref, v_ref, cols_ref, ccols_ref, rows_ref, h_ref, cst_ref, mst_ref,
                         c_scr, m_scr, *, n_chunks, ln):
    i = pl.program_id(0)

    @pl.when(i == 0)
    def _():
        c_scr[...] = jnp.zeros_like(c_scr)
        m_scr[...] = jnp.zeros_like(m_scr)

    lane = lax.broadcasted_iota(I32, (ln, LANES), 1)
    one_col = jnp.where(lane == 0, 1.0, 0.0).astype(F32)
    tril = lax.broadcasted_iota(I32, (ln, ln), 1) <= lax.broadcasted_iota(I32, (ln, ln), 0)
    srow = lax.broadcasted_iota(I32, (LANES, 2 * LANES), 0)

    def chunk(c, _):
        t0 = pl.multiple_of(c * ln, ln)
        cols = cols_ref[pl.ds(t0, ln), :]
        ccols = ccols_ref[pl.ds(t0, ln), :]
        rows = rows_ref[c]
        for p in range(H_M // 2):
            q2 = q_ref[pl.ds(t0, ln), p * LANES:(p + 1) * LANES]
            k2 = k_ref[pl.ds(t0, ln), p * LANES:(p + 1) * LANES].astype(BF16)
            c_pair = c_scr[p]
            c_pair_b = c_pair.astype(BF16)
            new_pair = c_pair
            for hh in range(2):
                h = 2 * p + hh
                own = (lane < DK_M) if hh == 0 else (lane >= DK_M)
                qm = jnp.where(own, q2, 0.0).astype(BF16)
                v = v_ref[pl.ds(t0, ln), h * DV_M:(h + 1) * DV_M]
                v_aug = jnp.concatenate([v, one_col], axis=1)
                ig_col = cols[:, 8 + h:9 + h]
                bc_col = ccols[:, 12 + h:13 + h]
                ig_row = rows[8 + h:9 + h, :]
                bc_row = rows[28 + h:29 + h, :]
                m0 = m_scr[h][0:1, 0:1]
                d = jnp.where(tril, bc_col - bc_row + ig_row, -jnp.inf)
                inter = bc_col + m0
                m_row = jnp.maximum(jnp.max(d, axis=1, keepdims=True), inter)
                s_in = jnp.exp(inter - m_row)
                sq = _dot_nt(qm, k2) * jnp.exp(d - m_row)
                num_aug = _dot(sq.astype(BF16), v_aug.astype(BF16)) + s_in * _dot(qm, c_pair_b)
                den = num_aug[:, DV_M:DV_M + 1]
                hv = num_aug[:, :DV_M] / jnp.maximum(jnp.abs(den), jnp.exp(-m_row))
                h_ref[pl.ds(t0, ln), h * DV_M:(h + 1) * DV_M] = hv
                b_last = bc_col[ln - 1:ln, :]
                dec = b_last - bc_col + ig_col
                m_new = jnp.maximum(b_last + m0, jnp.max(dec, axis=0, keepdims=True))
                ws = jnp.exp(dec - m_new)
                s0 = jnp.exp(b_last + m0 - m_new)
                upd = s0 * c_pair + _dot_tn(k2, (ws * v_aug).astype(BF16))
                rows_own = (srow < DK_M) if hh == 0 else (srow >= DK_M)
                new_pair = jnp.where(rows_own, upd, new_pair)
                m_scr[h] = jnp.broadcast_to(m_new, (8, LANES))
            c_scr[p] = new_pair
        return 0

    lax.fori_loop(0, n_chunks, chunk, 0)
    cst_ref[...] = c_scr[...]
    mst_ref[...] = m_scr[...]


def _mlstm_prompt(mq, mk, mv, cols, ccols, rows, g):
    s = mq.shape[0]
    ln = M_CHUNK
    tb = g * ln
    rows_ch = rows.reshape(32, s // ln, ln).transpose(1, 0, 2)
    rowb = lambda n: pl.BlockSpec((tb, n), lambda i: (i, 0))
    h, cst, mst = pl.pallas_call(
        functools.partial(_mlstm_prompt_kernel, n_chunks=g, ln=ln), grid=(s // tb,),
        in_specs=[rowb(256), rowb(256), rowb(512), rowb(LANES), rowb(LANES),
                  pl.BlockSpec((g, 32, ln), lambda i: (i, 0, 0))],
        out_specs=[rowb(512), pl.BlockSpec((2, LANES, 2 * LANES), lambda i: (0, 0, 0)),
                   pl.BlockSpec((H_M, 8, LANES), lambda i: (0, 0, 0))],
        out_shape=[jax.ShapeDtypeStruct((s, 512), F32), jax.ShapeDtypeStruct((2, LANES, 2 * LANES), F32),
                   jax.ShapeDtypeStruct((H_M, 8, LANES), F32)],
        scratch_shapes=[pltpu.VMEM((2, LANES, 2 * LANES), F32), pltpu.VMEM((H_M, 8, LANES), F32)],
        compiler_params=_cparams("arbitrary"), name="mlstm_prompt",
    )(mq, mk, mv, cols, ccols, rows_ch)
    c_new = cst[:, :, :DV_M].reshape(H_M, DK_M, DV_M)
    n_new = cst[:, :, DV_M].reshape(H_M, DK_M)
    m_new = mst[:, 0, 0]
    return h, c_new, n_new, m_new


def _mlstm_step_kernel(q_ref, k_ref, v_ref, cols_ref, c_ref, n_ref, m_ref, h_ref, cn_ref, nn_ref, mn_ref):
    cols = cols_ref[...]
    lane = lax.broadcasted_iota(I32, (1, LANES), 1)
    lane8 = lax.broadcasted_iota(I32, (8, LANES), 1)
    row8 = lax.broadcasted_iota(I32, (8, LANES), 0)
    srow = lax.broadcasted_iota(I32, (LANES, 1), 0)
    m_out = jnp.zeros((1, LANES), F32)
    for p in range(H_M // 2):
        q2 = q_ref[:, p * LANES:(p + 1) * LANES]
        k2 = k_ref[:, p * LANES:(p + 1) * LANES]
        q2r = q2.astype(BF16).astype(F32)
        k2r = k2.astype(BF16).astype(F32)
        n_pair = n_ref[:, p * LANES:(p + 1) * LANES]
        c_pair = c_ref[p * LANES:(p + 1) * LANES, :]
        c_pair_b = c_pair.astype(BF16)
        upd = jnp.zeros((LANES, LANES), F32)
        s0_rows = jnp.zeros((LANES, 1), F32)
        n_new = jnp.zeros((1, LANES), F32)
        for hh in range(2):
            h = 2 * p + hh
            own = (lane < DK_M) if hh == 0 else (lane >= DK_M)
            own8 = (lane8 < DK_M) if hh == 0 else (lane8 >= DK_M)
            qf = jnp.where(own, q2r, 0.0)
            v = v_ref[:, h * DV_M:(h + 1) * DV_M]
            ig = cols[:, 8 + h:9 + h]
            lf = cols[:, 12 + h:13 + h]
            m0 = m_ref[:, h:h + 1]
            inter = lf + m0
            m_row = jnp.maximum(ig, inter)
            s_in = jnp.exp(inter - m_row)
            wqk = jnp.sum(qf * k2r, axis=1, keepdims=True) * jnp.exp(ig - m_row)
            q_c = _dot(jnp.broadcast_to(qf, (8, LANES)).astype(BF16), c_pair_b)[0:1, :]
            q_n = jnp.sum(qf * n_pair.astype(BF16).astype(F32), axis=1, keepdims=True)
            num = wqk * v.astype(BF16).astype(F32) + s_in * q_c
            den = wqk + s_in * q_n
            h_ref[:, h * DV_M:(h + 1) * DV_M] = num / jnp.maximum(jnp.abs(den), jnp.exp(-m_row))
            ws = jnp.exp(ig - m_row)
            km8 = jnp.where(row8 == 0, jnp.where(own8, jnp.broadcast_to(k2r, (8, LANES)), 0.0), 0.0).astype(BF16)
            wv8 = jnp.where(row8 == 0, jnp.broadcast_to(ws * v, (8, LANES)), 0.0).astype(BF16)
            upd = upd + _dot_tn(km8, wv8)
            s0_rows = jnp.where((srow < DK_M) if hh == 0 else (srow >= DK_M), s_in, s0_rows)
            n_new = jnp.where(own, s_in * n_pair + ws * k2, n_new)
            m_out = jnp.where(lane == h, m_row, m_out)
        cn_ref[p * LANES:(p + 1) * LANES, :] = s0_rows * c_pair + upd
        nn_ref[:, p * LANES:(p + 1) * LANES] = n_new
    mn_ref[...] = m_out


def _mlstm_step(mq, mk, mv, cols, c0, n0, m0):
    bsz = mq.shape[0]
    b3 = lambda r, n: pl.BlockSpec((None, r, n), lambda i: (i, 0, 0))
    h, cn, nn, mn = pl.pallas_call(
        _mlstm_step_kernel, grid=(bsz,),
        in_specs=[b3(1, 256), b3(1, 256), b3(1, 512), b3(1, LANES), b3(H_M * DK_M, DV_M), b3(1, 256), b3(1, H_M)],
        out_specs=[b3(1, 512), b3(H_M * DK_M, DV_M), b3(1, 256), b3(1, LANES)],
        out_shape=[jax.ShapeDtypeStruct((bsz, 1, 512), F32), jax.ShapeDtypeStruct((bsz, H_M * DK_M, DV_M), F32),
                   jax.ShapeDtypeStruct((bsz, 1, 256), F32), jax.ShapeDtypeStruct((bsz, 1, LANES), F32)],
        compiler_params=_cparams("arbitrary"), name="mlstm_step",
    )(mq[:, None, :], mk[:, None, :], mv[:, None, :], cols[:, None, :],
      c0.reshape(bsz, H_M * DK_M, DV_M), n0.reshape(bsz, 1, H_M * DK_M), m0.reshape(bsz, 1, H_M))
    return (h[:, 0, :], cn.reshape(bsz, H_M, DK_M, DV_M), nn.reshape(bsz, H_M, DK_M), mn[:, 0, :H_M])


def _merge_kernel(x_ref, cy_ref, fy_ref, mh_ref, mo_ref, wg_ref, bg_ref, gm_ref, wb_ref, wo_ref, g_ref, b_ref, o_ref):
    x = x_ref[...]
    gates = jax.nn.sigmoid(_dot(x.astype(BF16), wg_ref[...]) + bg_ref[...])
    segs = []
    for h in range(H_M):
        seg = mh_ref[:, h * DV_M:(h + 1) * DV_M]
        segs.append(seg * lax.rsqrt(jnp.mean(seg * seg, axis=-1, keepdims=True) + LN_EPS))
    my = (jnp.concatenate(segs, axis=1) * gm_ref[...] * mo_ref[...]).astype(BF16)
    mix = gates[:, :D_MODEL] * _dot(cy_ref[...], wb_ref[0])
    mix = mix + gates[:, D_MODEL:2 * D_MODEL] * _dot(fy_ref[...], wb_ref[1])
    mix = mix + gates[:, 2 * D_MODEL:] * _dot(my, wb_ref[2])
    y = _dot(mix.astype(BF16), wo_ref[...])
    o_ref[...] = _layer_norm(ALPHA * x + y, g_ref[...], b_ref[...])


def _merge(x, conv_y, fox_y, m_h, m_o, w, tm):
    r = x.shape[0]
    rowb = lambda n: pl.BlockSpec((tm, n), lambda i: (i, 0))
    full = lambda a: pl.BlockSpec(a.shape, lambda i: (0,) * a.ndim)
    consts = [w['wg'], w['bg'], w['g_mnorm'], w['w_branch'], w['w_mix_out'], w['g_ln1'], w['b_ln1']]
    return pl.pallas_call(
        _merge_kernel, grid=(r // tm,),
        in_specs=[rowb(D_MODEL), rowb(512), rowb(512), rowb(512), rowb(512)] + [full(a) for a in consts],
        out_specs=rowb(D_MODEL), out_shape=jax.ShapeDtypeStruct((r, D_MODEL), F32),
        compiler_params=_cparams("arbitrary"), name="merge",
    )(x, conv_y, fox_y, m_h, m_o, *consts)


def _memkv_kernel(mem_ref, wk_ref, wv_ref, k_ref, v_ref, kb_ref, vb_ref):
    mb = mem_ref[...].astype(BF16)
    k = _dot(mb, wk_ref[...])
    v = _dot(mb, wv_ref[...])
    k_ref[...] = k
    v_ref[...] = v
    kb_ref[...] = k.astype(BF16)
    vb_ref[...] = v.astype(BF16)


def _memkv(mem, wk, wv):
    n = mem.shape[0]
    full = lambda a: pl.BlockSpec(a.shape, lambda i: (0,) * a.ndim)
    sd = lambda dt: jax.ShapeDtypeStruct((n, D_MODEL), dt)
    return pl.pallas_call(
        _memkv_kernel, grid=(1,),
        in_specs=[full(mem), full(wk), full(wv)],
        out_specs=[pl.BlockSpec((n, D_MODEL), lambda i: (0, 0))] * 4,
        out_shape=[sd(F32), sd(F32), sd(BF16), sd(BF16)],
        compiler_params=_cparams("arbitrary"), name="memkv",
    )(mem, wk, wv)


def _xattn_prompt_kernel(x_ref, kb_ref, vb_ref, wq_ref, wo_ref, g_ref, b_ref, o_ref):
    x = x_ref[...]
    q = (_dot(x.astype(BF16), wq_ref[...]) * DH_X ** -0.5).astype(BF16)
    outs = []
    for h in range(H_X):
        sl = slice(h * DH_X, (h + 1) * DH_X)
        s = _dot_nt(q[:, sl], kb_ref[:, sl])
        s = s - jnp.max(s, axis=1, keepdims=True)
        p = jnp.exp(s)
        o = _dot(p.astype(BF16), vb_ref[:, sl]) / jnp.sum(p, axis=1, keepdims=True)
        outs.append(o.astype(BF16))
    y = _dot(jnp.concatenate(outs, axis=1), wo_ref[...])
    o_ref[...] = _layer_norm(ALPHA * x + y, g_ref[...], b_ref[...])


def _xattn_prompt(x, kb, vb, w, tm):
    r = x.shape[0]
    rowb = pl.BlockSpec((tm, D_MODEL), lambda i: (i, 0))
    full = lambda a: pl.BlockSpec(a.shape, lambda i: (0,) * a.ndim)
    consts = [kb, vb, w['w_xq'], w['w_xo'], w['g_ln2'], w['b_ln2']]
    return pl.pallas_call(
        _xattn_prompt_kernel, grid=(r // tm,),
        in_specs=[rowb] + [full(a) for a in consts],
        out_specs=rowb, out_shape=jax.ShapeDtypeStruct((r, D_MODEL), F32),
        compiler_params=_cparams("arbitrary"), name="xattn_prompt",
    )(x, *consts)


def _linear_kernel(x_ref, w_ref, o_ref, *, scale):
    o_ref[...] = (_dot(x_ref[...].astype(BF16), w_ref[...]) * scale).astype(o_ref.dtype)


def _linear(x, w, scale, out_dtype):
    r, n = x.shape[0], w.shape[1]
    full = lambda a: pl.BlockSpec(a.shape, lambda i: (0,) * a.ndim)
    return pl.pallas_call(
        functools.partial(_linear_kernel, scale=scale), grid=(1,),
        in_specs=[full(x), full(w)], out_specs=pl.BlockSpec((r, n), lambda i: (0, 0)),
        out_shape=jax.ShapeDtypeStruct((r, n), out_dtype),
        compiler_params=_cparams("arbitrary"), name="linear",
    )(x, w)


def _linear_ln_kernel(x_ref, a_ref, w_ref, g_ref, b_ref, o_ref):
    y = _dot(a_ref[...].astype(BF16), w_ref[...])
    o_ref[...] = _layer_norm(ALPHA * x_ref[...] + y, g_ref[...], b_ref[...])


def _linear_ln(x, a, w, g, b):
    full = lambda t: pl.BlockSpec(t.shape, lambda i: (0,) * t.ndim)
    return pl.pallas_call(
        _linear_ln_kernel, grid=(1,),
        in_specs=[full(x), full(a), full(w), full(g), full(b)],
        out_specs=pl.BlockSpec(x.shape, lambda i: (0, 0)),
        out_shape=jax.ShapeDtypeStruct(x.shape, F32),
        compiler_params=_cparams("arbitrary"), name="linear_ln",
    )(x, a, w, g, b)


def _xattn_decode_kernel(q_ref, k_ref, v_ref, o_ref):
    w = q_ref.shape[-1]
    row = lax.broadcasted_iota(I32, (8, w), 0)
    lane = lax.broadcasted_iota(I32, (8, w), 1)
    own = (lane // DH_X) == row
    qrows = jnp.where(own, jnp.broadcast_to(q_ref[...].astype(F32), (8, w)), 0.0).astype(BF16)
    s = _dot_nt(qrows, k_ref[...].astype(BF16))
    s = s - jnp.max(s, axis=1, keepdims=True)
    p = jnp.exp(s)
    o = _dot(p.astype(BF16), v_ref[...].astype(BF16)) / jnp.sum(p, axis=1, keepdims=True)
    o_ref[...] = jnp.sum(jnp.where(own, o, 0.0), axis=0, keepdims=True)


def _xattn_decode(q, mem_k, mem_v):
    bsz, n_mem = mem_k.shape[0], mem_k.shape[1]
    b3 = lambda r: pl.BlockSpec((None, r, D_MODEL), lambda i: (i, 0, 0))
    out = pl.pallas_call(
        _xattn_decode_kernel, grid=(bsz,),
        in_specs=[b3(1), b3(n_mem), b3(n_mem)], out_specs=b3(1),
        out_shape=jax.ShapeDtypeStruct((bsz, 1, D_MODEL), F32),
        compiler_params=_cparams("arbitrary"), name="xattn_decode",
    )(q[:, None, :], mem_k.reshape(bsz, n_mem, D_MODEL), mem_v.reshape(bsz, n_mem, D_MODEL))
    return out[:, 0, :]


def _router_kernel(x_ref, w_ref, b_ref, lst_ref, ti_ref, tg_ref, cnt_ref, carry_ref):
    i = pl.program_id(0)

    @pl.when(i == 0)
    def _():
        carry_ref[...] = jnp.zeros_like(carry_ref)

    x1, x2, x3 = _split3(x_ref[...])
    w1, w2, w3 = w_ref[0], w_ref[1], w_ref[2]
    logits = (_dot(x1, w1) + (_dot(x1, w2) + _dot(x2, w1)) + (_dot(x1, w3) + _dot(x2, w2) + _dot(x3, w1))) + b_ref[...]
    tm = logits.shape[0]
    lane = lax.broadcasted_iota(I32, (tm, LANES), 1)
    lane_f = lane.astype(F32)
    vals = logits
    tops, idxs, sels = [], [], []
    for _ in range(TOP_K):
        mx = jnp.max(vals, axis=1, keepdims=True)
        idx = jnp.min(jnp.where(vals == mx, lane_f, float(LANES)), axis=1, keepdims=True)
        sel = lane_f == idx
        tops.append(mx)
        idxs.append(idx)
        sels.append(sel)
        vals = jnp.where(sel, -jnp.inf, vals)
    exps = [jnp.exp(t - tops[0]) for t in tops]
    den = exps[0] + exps[1] + exps[2] + exps[3]
    cnt = jnp.zeros((tm, LANES), F32)
    for sel in sels:
        cnt = cnt + jnp.where(sel, 1.0, 0.0)
    excl = _dot(lst_ref[...], cnt.astype(BF16)) + carry_ref[0:1, :]
    ti = jnp.zeros((tm, LANES), F32)
    tg = jnp.zeros((tm, LANES), F32)
    for k in range(TOP_K):
        rank = jnp.sum(jnp.where(sels[k], excl, 0.0), axis=1, keepdims=True)
        ti = jnp.where(lane == k, idxs[k], ti)
        ti = jnp.where(lane == TOP_K + k, rank, ti)
        tg = jnp.where(lane == k, exps[k] / den, tg)
    ti_ref[...] = ti.astype(I32)
    tg_ref[...] = tg
    carry_ref[...] = carry_ref[...] + jnp.sum(cnt, axis=0, keepdims=True)
    cnt_ref[...] = carry_ref[...]


def _router(x, w3, b, tm):
    r = x.shape[0]
    t = np.arange(tm)
    lst = jnp.asarray(t[:, None] > t[None, :], BF16)
    full = lambda a: pl.BlockSpec(a.shape, lambda i: (0,) * a.ndim)
    rowb = lambda n: pl.BlockSpec((tm, n), lambda i: (i, 0))
    return pl.pallas_call(
        _router_kernel, grid=(r // tm,),
        in_specs=[rowb(D_MODEL), full(w3), full(b), full(lst)],
        out_specs=[rowb(LANES), rowb(LANES), pl.BlockSpec((8, LANES), lambda i: (0, 0))],
        out_shape=[jax.ShapeDtypeStruct((r, LANES), I32), jax.ShapeDtypeStruct((r, LANES), F32),
                   jax.ShapeDtypeStruct((8, LANES), F32)],
        scratch_shapes=[pltpu.VMEM((8, LANES), F32)],
        compiler_params=_cparams("arbitrary"), name="router",
    )(x, w3, b, lst)


def _moe_dispatch_kernel(idx_hbm, x_ref, xs_in, xs_hbm, idx_smem, sem_idx, sem):
    del xs_in
    i = pl.program_id(0)
    tm = x_ref.shape[0]
    idx_cp = pltpu.make_async_copy(idx_hbm.at[i], idx_smem, sem_idx)
    idx_cp.start()
    idx_cp.wait()

    def send(r, _):
        for k in range(TOP_K):
            pltpu.make_async_copy(x_ref.at[pl.ds(r, 1)], xs_hbm.at[pl.ds(idx_smem[r * TOP_K + k], 1)], sem).start()
        return 0

    lax.fori_loop(0, tm, send, 0)
    for k in range(TOP_K):
        pltpu.make_async_copy(x_ref, xs_hbm.at[pl.ds(0, tm)], sem).wait()


def _moe_dispatch(idx, x, rows, tm):
    n = x.shape[0]
    return pl.pallas_call(
        _moe_dispatch_kernel, grid=(n // tm,),
        in_specs=[pl.BlockSpec(memory_space=pl.ANY), pl.BlockSpec((tm, D_MODEL), lambda i: (i, 0)),
                  pl.BlockSpec(memory_space=pl.ANY)],
        out_specs=pl.BlockSpec(memory_space=pl.ANY),
        out_shape=jax.ShapeDtypeStruct((rows, D_MODEL), F32),
        scratch_shapes=[pltpu.SMEM((tm * TOP_K,), I32), pltpu.SemaphoreType.DMA(()), pltpu.SemaphoreType.DMA(())],
        input_output_aliases={2: 0},
        compiler_params=_cparams("arbitrary"), name="moe_dispatch",
    )(idx, x, jnp.zeros((rows, D_MODEL), F32))


def _moe_ffn_kernel(be_ref, nv_ref, xs_ref, wgu_ref, bgu_ref, wd_ref, bd_ref, y_ref, wgu_b, wd_b):
    i = pl.program_id(0)

    @pl.when(i < nv_ref[0])
    def _():
        e = be_ref[i]
        e_prev = be_ref[jnp.maximum(i - 1, 0)]

        @pl.when((i == 0) | (e != e_prev))
        def _():
            wgu_b[...] = wgu_ref[...].astype(BF16)
            wd_b[...] = wd_ref[...].astype(BF16)

        gu = _dot(xs_ref[...].astype(BF16), wgu_b[...]) + bgu_ref[...]
        g = jnp.minimum(gu[:, :D_EXPERT], SWIGLU_LIMIT)
        u = jnp.clip(gu[:, D_EXPERT:], -SWIGLU_LIMIT, SWIGLU_LIMIT)
        act = (u + 1.0) * g * jax.nn.sigmoid(SWIGLU_ALPHA * g)
        y_ref[...] = _dot(act.astype(BF16), wd_b[...]) + bd_ref[...]

    @pl.when(i >= nv_ref[0])
    def _():
        y_ref[...] = jnp.zeros_like(y_ref)


def _moe_ffn(blk_e, n_valid, xs, w, l, tb):
    rows = xs.shape[0]
    wsel = lambda i, be, nv: (l, be[i], 0, 0)
    grid_spec = pltpu.PrefetchScalarGridSpec(
        num_scalar_prefetch=2, grid=(rows // tb,),
        in_specs=[pl.BlockSpec((tb, D_MODEL), lambda i, be, nv: (jnp.minimum(i, nv[0] - 1), 0)),
                  pl.BlockSpec((None, None, D_MODEL, 2 * D_EXPERT), wsel),
                  pl.BlockSpec((None, None, 1, 2 * D_EXPERT), wsel),
                  pl.BlockSpec((None, None, D_EXPERT, D_MODEL), wsel),
                  pl.BlockSpec((None, None, 1, D_MODEL), wsel)],
        out_specs=pl.BlockSpec((tb, D_MODEL), lambda i, be, nv: (i, 0)),
        scratch_shapes=[pltpu.VMEM((D_MODEL, 2 * D_EXPERT), BF16), pltpu.VMEM((D_EXPERT, D_MODEL), BF16)])
    return pl.pallas_call(
        _moe_ffn_kernel, grid_spec=grid_spec,
        out_shape=jax.ShapeDtypeStruct((rows, D_MODEL), F32),
        compiler_params=_cparams("arbitrary"), name="moe_ffn",
    )(blk_e, n_valid, xs, w['w_gate_up'], w['b_gate_up'][:, :, None, :], w['w_down'], w['b_down'][:, :, None, :])


def _moe_combine_kernel(idx_hbm, x_ref, tg_ref, ys_hbm, g_ref, b_ref, o_ref, idx_smem, ybuf, sem_idx, sem):
    i = pl.program_id(0)
    tm = x_ref.shape[0]
    idx_cp = pltpu.make_async_copy(idx_hbm.at[i], idx_smem, sem_idx)
    idx_cp.start()
    idx_cp.wait()

    def fetch(r, _):
        for k in range(TOP_K):
            pltpu.make_async_copy(ys_hbm.at[pl.ds(idx_smem[r * TOP_K + k], 1)], ybuf.at[k, pl.ds(r, 1)], sem).start()
        return 0

    lax.fori_loop(0, tm, fetch, 0)
    for k in range(TOP_K):
        pltpu.make_async_copy(ys_hbm.at[pl.ds(0, tm)], ybuf.at[k], sem).wait()
    tg = tg_ref[...]
    ff = tg[:, 0:1] * ybuf[0]
    for k in range(1, TOP_K):
        ff = ff + tg[:, k:k + 1] * ybuf[k]
    o_ref[...] = _layer_norm(ALPHA * x_ref[...] + ff, g_ref[...], b_ref[...])


def _moe_combine(idx, x, tg, ys, g, b, tm):
    n = x.shape[0]
    rowb = lambda wd: pl.BlockSpec((tm, wd), lambda i: (i, 0))
    full = lambda t: pl.BlockSpec(t.shape, lambda i: (0,) * t.ndim)
    return pl.pallas_call(
        _moe_combine_kernel, grid=(n // tm,),
        in_specs=[pl.BlockSpec(memory_space=pl.ANY), rowb(D_MODEL), rowb(LANES), pl.BlockSpec(memory_space=pl.ANY),
                  full(g), full(b)],
        out_specs=rowb(D_MODEL), out_shape=jax.ShapeDtypeStruct((n, D_MODEL), F32),
        scratch_shapes=[pltpu.SMEM((tm * TOP_K,), I32), pltpu.VMEM((TOP_K, tm, D_MODEL), F32),
                        pltpu.SemaphoreType.DMA(()), pltpu.SemaphoreType.DMA(())],
        compiler_params=_cparams("arbitrary"), name="moe_combine",
    )(idx, x, tg, ys, g, b)


def _moe(x, w, l, tb, tm):
    n = x.shape[0]
    a = n * TOP_K
    ti, tg, cnt = _router(x, w['w_router3'], w['b_router'], tm)
    top_e = ti[:, :TOP_K]
    rank = ti[:, TOP_K:2 * TOP_K]
    counts = cnt[0, :N_EXPERTS].astype(I32)
    n_blk = -(-a // tb) + N_EXPERTS
    padded = (counts + tb - 1) // tb * tb
    pend = jnp.cumsum(padded)
    dest = (pend - padded)[top_e] + rank
    idx = dest.reshape(n // tm, tm * TOP_K)
    n_valid = (pend[-1:] // tb).astype(I32)
    blk_id = jnp.minimum(jnp.arange(n_blk, dtype=I32), n_valid - 1)
    blk_e = jnp.minimum(jnp.sum((pend[None, :] <= (blk_id * tb)[:, None]).astype(I32), axis=1), N_EXPERTS - 1)
    xs = _moe_dispatch(idx, x, n_blk * tb, tm)
    ys = _moe_ffn(blk_e, n_valid, xs, w, l, tb)
    return _moe_combine(idx, x, tg, ys, w['g_ln3'], w['b_ln3'], tm)


def _layer_weights(l, w_in, b_in, b_fox_f, b_mlstm_f, g_mlstm_norm, w_branch, w_mix_out, g_ln1, b_ln1,
                   w_xq, w_xk, w_xv, w_xo, g_ln2, b_ln2, w_router, b_router, w_gate_up, b_gate_up,
                   w_down, b_down, g_ln3, b_ln3):
    w = _split_w_in(w_in[l], b_in[l], b_fox_f[l], b_mlstm_f[l])
    wr = jnp.pad(w_router[l], ((0, 0), (0, LANES - N_EXPERTS)))
    r1 = wr.astype(BF16)
    r2 = (wr - r1.astype(F32)).astype(BF16)
    r3 = (wr - r1.astype(F32) - r2.astype(F32)).astype(BF16)
    w.update(
        g_mnorm=g_mlstm_norm[l][None], w_branch=w_branch[l].astype(BF16), w_mix_out=w_mix_out[l].astype(BF16),
        g_ln1=g_ln1[l][None], b_ln1=b_ln1[l][None],
        w_xq=w_xq[l].astype(BF16), w_xk=w_xk[l].astype(BF16), w_xv=w_xv[l].astype(BF16), w_xo=w_xo[l].astype(BF16),
        g_ln2=g_ln2[l][None], b_ln2=b_ln2[l][None],
        w_router3=jnp.stack([r1, r2, r3]),
        b_router=jnp.pad(b_router[l], (0, LANES - N_EXPERTS), constant_values=NEG_BIG)[None],
        w_gate_up=w_gate_up, b_gate_up=b_gate_up, w_down=w_down, b_down=b_down,
        g_ln3=g_ln3[l][None], b_ln3=b_ln3[l][None])
    return w


def kernel(x_prompt, x_sample, mem_prompt, cache_fox_k, cache_fox_v, cache_fox_lf, page_table, state_conv,
           state_mlstm_c, state_mlstm_n, state_mlstm_m, cache_mem_k, cache_mem_v, w_in, b_in, b_fox_f, b_mlstm_f,
           w_dw, b_dw, g_conv_ln, b_conv_ln, g_mlstm_norm, w_branch, w_mix_out, g_ln1, b_ln1, w_xq, w_xk, w_xv,
           w_xo, g_ln2, b_ln2, w_router, b_router, w_gate_up, b_gate_up, w_down, b_down, g_ln3, b_ln3):
    b_p, s, _ = x_prompt.shape
    b_s = x_sample.shape[0]
    assert b_p == 1 and x_sample.shape[1] == 1
    depth = w_in.shape[0]
    xp = x_prompt.reshape(s, D_MODEL)
    xs = x_sample.reshape(b_s, D_MODEL)
    mem = mem_prompt.reshape(-1, D_MODEL)
    tm = _row_block(s, 256)
    tq = _row_block(s, 256)
    tk = _row_block(s, 512)
    g_chunks = 8 if s % (8 * M_CHUNK) == 0 else 1
    tb_p = 256 if s * TOP_K >= 256 * N_EXPERTS else 8
    rows_p, rows_s = [], []
    for l in range(depth):
        w = _layer_weights(l, w_in, b_in, b_fox_f, b_mlstm_f, g_mlstm_norm, w_branch, w_mix_out, g_ln1, b_ln1,
                           w_xq, w_xk, w_xv, w_xo, g_ln2, b_ln2, w_router, b_router, w_gate_up, b_gate_up,
                           w_down, b_down, g_ln3, b_ln3)
        pr = _inproj(xp, w, tm)
        conv_y = _conv_prompt(pr['u'], w_dw[l], b_dw[l], g_conv_ln[l], b_conv_ln[l], tm)
        fox_y = _fox_prompt(pr['fqb'], pr['fkb'], pr['fvb'], pr['rows'][16:24], tq, tk)
        m_h, m_c, m_n, m_m = _mlstm_prompt(pr['mq'], pr['mk'], pr['mv'], pr['cols'], pr['ccols'], pr['rows'], g_chunks)
        xp = _merge(xp, conv_y, fox_y, m_h, pr['mo'], w, tm)
        mem_k, mem_v, mem_kb, mem_vb = _memkv(mem, w['w_xk'], w['w_xv'])
        xp = _xattn_prompt(xp, mem_kb, mem_vb, w, tm)
        xp = _moe(xp, w, l, tb_p, tm)
        rows_p.append((pr['fk'].reshape(1, s, H_F, DH_F), pr['fv'].reshape(1, s, H_F, DH_F),
                       pr['cols'][:, :H_F].reshape(1, s, H_F), pr['u'][s - (K_CONV - 1):][None],
                       m_c[None], m_n[None], m_m[None],
                       mem_k.reshape(1, -1, H_X, DH_X), mem_v.reshape(1, -1, H_X, DH_X)))
        sr = _inproj(xs, w, b_s)
        conv_y, conv_st = _conv_step(state_conv[l], sr['u'], w_dw[l], b_dw[l], g_conv_ln[l], b_conv_ln[l])
        lf_new = sr['cols'][:, :H_F]
        fox_y = _fox_decode(page_table, sr['fqb'], sr['fk'], sr['fv'], lf_new, cache_fox_k, cache_fox_v,
                            cache_fox_lf, l)
        m_h, m_c, m_n, m_m = _mlstm_step(sr['mq'], sr['mk'], sr['mv'], sr['cols'], state_mlstm_c[l],
                                         state_mlstm_n[l], state_mlstm_m[l])
        xs = _merge(xs, conv_y, fox_y.astype(BF16), m_h, sr['mo'], w, b_s)
        q = _linear(xs, w['w_xq'], DH_X ** -0.5, BF16)
        att = _xattn_decode(q, cache_mem_k[l], cache_mem_v[l])
        xs = _linear_ln(xs, att, w['w_xo'], w['g_ln2'], w['b_ln2'])
        xs = _moe(xs, w, l, 8, b_s)
        rows_s.append((sr['fk'].reshape(b_s, 1, H_F, DH_F), sr['fv'].reshape(b_s, 1, H_F, DH_F),
                       lf_new.reshape(b_s, 1, H_F), conv_st, m_c, m_n, m_m))
    outs_p = [jnp.stack(a) for a in zip(*rows_p)]
    outs_s = [jnp.stack(a) for a in zip(*rows_s)]
    return (xp.reshape(1, s, D_MODEL), xs.reshape(b_s, 1, D_MODEL), *outs_p, *outs_s)
```

```python
import functools

import numpy as np
import jax
import jax.numpy as jnp
from jax import lax
from jax.experimental import pallas as pl
from jax.experimental.pallas import tpu as pltpu

F32 = jnp.float32
BF16 = jnp.bfloat16
I32 = jnp.int32

D_MODEL = 1024
W_BR = 512
K_CONV = 31
H_F = 8
DH_F = 64
H_M = 4
DK_M = 64
DV_M = 128
M_CHUNK = 64
I_CAP = 15.0
H_X = 4
DH_X = 256
N_EXPERTS = 32
TOP_K = 4
D_EXPERT = 1024
SWIGLU_ALPHA = 1.702
SWIGLU_LIMIT = 7.0
LN_EPS = 1e-5
DEPTH = 2
ALPHA = (2.0 * DEPTH) ** 0.25
NEG_BIG = -1e30
LANES = 128
VMEM_LIMIT = 56 * 1024 * 1024


def _cparams(*sem):
    return pltpu.CompilerParams(dimension_semantics=sem, vmem_limit_bytes=VMEM_LIMIT)


def _dot(a, b):
    return jnp.dot(a, b, preferred_element_type=F32)


def _dot_nt(a, b):
    return lax.dot_general(a, b, (((1,), (1,)), ((), ())), preferred_element_type=F32)


def _dot_tn(a, b):
    return lax.dot_general(a, b, (((0,), (0,)), ((), ())), preferred_element_type=F32)


def _split3(a):
    a1 = a.astype(BF16)
    r = a - a1.astype(F32)
    a2 = r.astype(BF16)
    r = r - a2.astype(F32)
    return a1, a2, r.astype(BF16)


def _log_sigmoid(x):
    return jnp.minimum(x, 0.0) - jnp.log1p(jnp.exp(-jnp.abs(x)))


def _layer_norm(x, g, b):
    mu = jnp.mean(x, axis=-1, keepdims=True)
    xc = x - mu
    var = jnp.mean(xc * xc, axis=-1, keepdims=True)
    return xc * lax.rsqrt(var + LN_EPS) * g + b


def _row_block(n, pref):
    return pref if n % pref == 0 else n


def _gate_act(z, idx):
    ls = _log_sigmoid(z)
    tg = I_CAP * jnp.tanh(z * (1.0 / I_CAP))
    return jnp.where(idx < 8, ls, jnp.where(idx < 12, tg, jnp.where(idx < 16, ls, 0.0)))


def _inproj_kernel(x_ref, wc_ref, bc_ref, wf_ref, bf_ref, wm_ref, bm_ref, ws_ref, bs_ref, wst_ref, bst_ref,
                   ufull_ref, ublk_ref, lblk_ref,
                   u_ref, fqb_ref, fk_ref, fv_ref, fkb_ref, fvb_ref, mq_ref, mk_ref, mv_ref, mo_ref,
                   cols_ref, ccols_ref, rows_ref, carry_ref):
    i = pl.program_id(0)

    @pl.when(i == 0)
    def _():
        carry_ref[...] = jnp.zeros_like(carry_ref)

    tm = x_ref.shape[0]
    xb = x_ref[...].astype(BF16)
    zc = _dot(xb, wc_ref[...]) + bc_ref[...]
    u_ref[...] = zc[:, :W_BR] * jax.nn.sigmoid(zc[:, W_BR:])

    zf = _dot(xb, wf_ref[...]) + bf_ref[...]
    fk = zf[:, W_BR:2 * W_BR]
    fv = zf[:, 2 * W_BR:]
    fqb_ref[...] = (zf[:, :W_BR] * DH_F ** -0.5).astype(BF16)
    fk_ref[...] = fk
    fv_ref[...] = fv
    fkb_ref[...] = fk.astype(BF16)
    fvb_ref[...] = fv.astype(BF16)

    zm = _dot(xb, wm_ref[...]) + bm_ref[...]
    mq_ref[...] = zm[:, :256]
    mk_ref[...] = zm[:, 256:512] * DK_M ** -0.5
    mv_ref[...] = zm[:, 512:1024]
    mo_ref[...] = jax.nn.sigmoid(zm[:, 1024:])

    zs = _dot(xb, ws_ref[...]) + bs_ref[...]
    lane = lax.broadcasted_iota(I32, zs.shape, 1)
    act_c = _gate_act(zs, lane)
    cols_ref[...] = act_c
    c1, c2, c3 = _split3(act_c)
    lb = lblk_ref[...]
    ccols_ref[...] = _dot(lb, c1) + _dot(lb, c2) + _dot(lb, c3)

    zt = _dot_nt(wst_ref[...], xb) + bst_ref[...]
    row = lax.broadcasted_iota(I32, zt.shape, 0)
    act_r = _gate_act(zt, row)
    r1, r2, r3 = _split3(act_r)
    uf = ufull_ref[...]
    ub = ublk_ref[...]
    cum_full = _dot(r1, uf) + _dot(r2, uf) + _dot(r3, uf) + carry_ref[:, 0:1]
    cum_blk = _dot(r1, ub) + _dot(r2, ub) + _dot(r3, ub)
    carry_ref[...] = jnp.broadcast_to(cum_full[:, tm - 1:tm], carry_ref.shape)
    rows_ref[0:16, :] = act_r
    rows_ref[16:32, :] = jnp.where(row < 8, cum_full, cum_blk)


def _tri_consts(tm, chunk):
    t = np.arange(tm)
    upper = (t[:, None] <= t[None, :])
    same = (t[:, None] // chunk) == (t[None, :] // chunk)
    ufull = jnp.asarray(upper, BF16)
    ublk = jnp.asarray(upper & same, BF16)
    lblk = jnp.asarray(upper.T & same, BF16)
    return ufull, ublk, lblk


def _split_w_in(w_in, b_in, b_fox_f, b_mlstm_f):
    o = np.cumsum([0, 512, 512, 512, 512, 512, 8, 256, 256, 512, 4, 4, 512, 3072])
    sl = lambda a, i, j: a[..., o[i]:o[j]]
    wc, bc = sl(w_in, 0, 2), sl(b_in, 0, 2)
    wf, bf = sl(w_in, 2, 5), sl(b_in, 2, 5)
    wm = jnp.concatenate([sl(w_in, 6, 9), sl(w_in, 11, 12)], axis=-1)
    bm = jnp.concatenate([sl(b_in, 6, 9), sl(b_in, 11, 12)], axis=-1)
    wsm = jnp.concatenate([sl(w_in, 5, 6), sl(w_in, 9, 11)], axis=-1)
    bsm = jnp.concatenate([sl(b_in, 5, 6) + b_fox_f, sl(b_in, 9, 10), sl(b_in, 10, 11) + b_mlstm_f], axis=-1)
    ws = jnp.pad(wsm, ((0, 0), (0, LANES - 16)))
    bs = jnp.pad(bsm, (0, LANES - 16))
    wg, bg = sl(w_in, 12, 13), sl(b_in, 12, 13)
    return dict(wc=wc.astype(BF16), bc=bc[None], wf=wf.astype(BF16), bf=bf[None], wm=wm.astype(BF16), bm=bm[None],
                ws=ws.astype(BF16), bs=bs[None], wst=wsm.T.astype(BF16), bst=bsm[:, None],
                wg=wg.astype(BF16), bg=bg[None])


def _inproj(x, w, tm):
    r = x.shape[0]
    chunk = min(M_CHUNK, tm)
    ufull, ublk, lblk = _tri_consts(tm, chunk)
    rowb = lambda n: pl.BlockSpec((tm, n), lambda i: (i, 0))
    full = lambda a: pl.BlockSpec(a.shape, lambda i: (0,) * a.ndim)
    consts = [w['wc'], w['bc'], w['wf'], w['bf'], w['wm'], w['bm'], w['ws'], w['bs'], w['wst'], w['bst'],
              ufull, ublk, lblk]
    out_shape = [jax.ShapeDtypeStruct((r, 512), F32), jax.ShapeDtypeStruct((r, 512), BF16),
                 jax.ShapeDtypeStruct((r, 512), F32), jax.ShapeDtypeStruct((r, 512), F32),
                 jax.ShapeDtypeStruct((r, 512), BF16), jax.ShapeDtypeStruct((r, 512), BF16),
                 jax.ShapeDtypeStruct((r, 256), F32), jax.ShapeDtypeStruct((r, 256), F32),
                 jax.ShapeDtypeStruct((r, 512), F32), jax.ShapeDtypeStruct((r, 512), F32),
                 jax.ShapeDtypeStruct((r, LANES), F32), jax.ShapeDtypeStruct((r, LANES), F32),
                 jax.ShapeDtypeStruct((32, r), F32)]
    out_specs = [rowb(512)] * 6 + [rowb(256)] * 2 + [rowb(512)] * 2 + [rowb(LANES)] * 2 + \
                [pl.BlockSpec((32, tm), lambda i: (0, i))]
    outs = pl.pallas_call(
        _inproj_kernel, grid=(r // tm,),
        in_specs=[rowb(D_MODEL)] + [full(a) for a in consts],
        out_specs=out_specs, out_shape=out_shape,
        scratch_shapes=[pltpu.VMEM((16, LANES), F32)],
        compiler_params=_cparams("arbitrary"), name="inproj",
    )(x, *consts)
    keys = ['u', 'fqb', 'fk', 'fv', 'fkb', 'fvb', 'mq', 'mk', 'mv', 'mo', 'cols', 'ccols', 'rows']
    return dict(zip(keys, outs))


HALO = 32


def _conv_kernel(u_ref, w_ref, b_ref, g_ref, bln_ref, y_ref, buf_ref):
    i = pl.program_id(0)
    tm = u_ref.shape[0]

    @pl.when(i == 0)
    def _():
        buf_ref[0:HALO, :] = jnp.zeros((HALO, W_BR), F32)

    buf_ref[HALO:HALO + tm, :] = u_ref[...]
    acc = jnp.zeros((tm, W_BR), F32) + b_ref[...]
    off = HALO - (K_CONV - 1)
    for k in range(K_CONV):
        acc = acc + w_ref[k:k + 1, :] * buf_ref[off + k:off + k + tm, :]
    yn = _layer_norm(acc, g_ref[...], bln_ref[...])
    y_ref[...] = (yn * jax.nn.sigmoid(yn)).astype(y_ref.dtype)
    buf_ref[0:HALO, :] = buf_ref[tm:tm + HALO, :]


def _conv_prompt(u, w_dw, b_dw, g, b, tm):
    s = u.shape[0]
    full = lambda a: pl.BlockSpec(a.shape, lambda i: (0,) * a.ndim)
    args = [w_dw, b_dw[None], g[None], b[None]]
    return pl.pallas_call(
        _conv_kernel, grid=(s // tm,),
        in_specs=[pl.BlockSpec((tm, W_BR), lambda i: (i, 0))] + [full(a) for a in args],
        out_specs=pl.BlockSpec((tm, W_BR), lambda i: (i, 0)),
        out_shape=jax.ShapeDtypeStruct((s, W_BR), BF16),
        scratch_shapes=[pltpu.VMEM((HALO + tm, W_BR), F32)],
        compiler_params=_cparams("arbitrary"), name="conv_prompt",
    )(u, *args)


def _conv_step_kernel(st_ref, u_ref, w_ref, b_ref, g_ref, bln_ref, y_ref, nst_ref):
    st = st_ref[0]
    un = u_ref[0]
    acc = b_ref[...] + w_ref[K_CONV - 1:K_CONV, :] * un
    acc = acc + jnp.sum(w_ref[0:K_CONV - 1, :] * st, axis=0, keepdims=True)
    yn = _layer_norm(acc, g_ref[...], bln_ref[...])
    y_ref[0] = (yn * jax.nn.sigmoid(yn)).astype(y_ref.dtype)
    nst_ref[0, 0:K_CONV - 2, :] = st[1:, :]
    nst_ref[0, K_CONV - 2:K_CONV - 1, :] = un


def _conv_step(state, u, w_dw, b_dw, g, b):
    bsz = u.shape[0]
    full = lambda a: pl.BlockSpec(a.shape, lambda i: (0,) * a.ndim)
    args = [w_dw, b_dw[None], g[None], b[None]]
    y, nst = pl.pallas_call(
        _conv_step_kernel, grid=(bsz,),
        in_specs=[pl.BlockSpec((1, K_CONV - 1, W_BR), lambda i: (i, 0, 0)),
                  pl.BlockSpec((1, 1, W_BR), lambda i: (i, 0, 0))] + [full(a) for a in args],
        out_specs=[pl.BlockSpec((1, 1, W_BR), lambda i: (i, 0, 0)),
                   pl.BlockSpec((1, K_CONV - 1, W_BR), lambda i: (i, 0, 0))],
        out_shape=[jax.ShapeDtypeStruct((bsz, 1, W_BR), BF16),
                   jax.ShapeDtypeStruct((bsz, K_CONV - 1, W_BR), F32)],
        compiler_params=_cparams("arbitrary"), name="conv_step",
    )(state, u[:, None, :], *args)
    return y[:, 0, :], nst


def _fox_prompt_kernel(q_ref, k_ref, v_ref, c_ref, o_ref, s_a, s_b, p_a, p_b, m_s, l_s, a_s, acc_s, *, tq, tk):
    qi = pl.program_id(1)
    q2 = q_ref[...].astype(F32)
    lane = lax.broadcasted_iota(I32, (tq, LANES), 1)
    q_start = pl.multiple_of(qi * tq, tq)
    n_full = (qi * tq) // tk
    qms = [jnp.where(lane < DH_F, q2, 0.0).astype(BF16), jnp.where(lane < DH_F, 0.0, q2).astype(BF16)]
    c0s = [c_ref[hh:hh + 1, pl.ds(q_start, LANES)][:, 0:1] for hh in range(2)]

    def scores(j, s_ref):
        k_start = pl.multiple_of(j * tk, tk)
        kb = k_ref[pl.ds(k_start, tk), :]
        for hh in range(2):
            s_ref[hh] = _dot_nt(qms[hh], kb) + (c0s[hh] - c_ref[hh:hh + 1, pl.ds(k_start, tk)])

    def values(j, p_ref):
        k_start = pl.multiple_of(j * tk, tk)
        vb = v_ref[pl.ds(k_start, tk), :]
        for hh in range(2):
            acc_s[hh] = a_s[hh] * acc_s[hh] + _dot(p_ref[hh], vb)

    def softmax(j, s_ref, p_ref, masked):
        for hh in range(2):
            s = s_ref[hh]
            if masked:
                rows = q_start + lax.broadcasted_iota(I32, (tq, tk), 0)
                cols = j * tk + lax.broadcasted_iota(I32, (tq, tk), 1)
                s = jnp.where(cols <= rows, s, NEG_BIG)
            m_old = m_s[hh]
            m_new = jnp.maximum(m_old, jnp.max(s, axis=1, keepdims=True))
            a = jnp.exp(m_old - m_new)
            p = jnp.exp(s - m_new)
            p_ref[hh] = p.astype(BF16)
            m_s[hh] = m_new
            l_s[hh] = a * l_s[hh] + jnp.sum(p, axis=1, keepdims=True)
            a_s[hh] = a

    def stage(j, s_cur, s_nxt, p_cur, p_prev, last):
        if not last:
            scores(j + 1, s_nxt)
        values(jnp.maximum(j - 1, 0), p_prev)
        softmax(j, s_cur, p_cur, last)
        if last:
            values(j, p_cur)

    def by_parity(j, last):
        even = lax.rem(j, 2) == 0
        pl.when(even)(functools.partial(stage, j, s_a, s_b, p_a, p_b, last))
        pl.when(jnp.logical_not(even))(functools.partial(stage, j, s_b, s_a, p_b, p_a, last))

    p_b[...] = jnp.zeros_like(p_b)
    m_s[...] = jnp.full_like(m_s, NEG_BIG)
    l_s[...] = jnp.zeros_like(l_s)
    a_s[...] = jnp.ones_like(a_s)
    acc_s[...] = jnp.zeros_like(acc_s)
    scores(0, s_a)

    def body(j, _):
        by_parity(j, False)
        return 0

    lax.fori_loop(0, n_full, body, 0)
    by_parity(n_full, True)
    o_ref[...] = jnp.where(lane < DH_F, acc_s[0] / l_s[0], acc_s[1] / l_s[1]).astype(o_ref.dtype)


def _fox_prompt(fqb, fkb, fvb, crow, tq, tk):
    s = fqb.shape[0]
    c3 = crow.reshape(H_F // 2, 2, s)
    return pl.pallas_call(
        functools.partial(_fox_prompt_kernel, tq=tq, tk=tk), grid=(H_F // 2, s // tq),
        in_specs=[pl.BlockSpec((tq, LANES), lambda p, i: (i, p)),
                  pl.BlockSpec((s, LANES), lambda p, i: (0, p)),
                  pl.BlockSpec((s, LANES), lambda p, i: (0, p)),
                  pl.BlockSpec((None, 2, s), lambda p, i: (p, 0, 0))],
        out_specs=pl.BlockSpec((tq, LANES), lambda p, i: (i, p)),
        out_shape=jax.ShapeDtypeStruct((s, W_BR), BF16),
        scratch_shapes=[pltpu.VMEM((2, tq, tk), F32), pltpu.VMEM((2, tq, tk), F32),
                        pltpu.VMEM((2, tq, tk), BF16), pltpu.VMEM((2, tq, tk), BF16),
                        pltpu.VMEM((2, tq, 1), F32), pltpu.VMEM((2, tq, 1), F32), pltpu.VMEM((2, tq, 1), F32),
                        pltpu.VMEM((2, tq, LANES), F32)],
        compiler_params=_cparams("arbitrary", "arbitrary"), name="fox_prompt",
    )(fqb, fkb, fvb, c3)


def _largest_divisor(n, options):
    return next(o for o in options if n % o == 0)


def _fox_suffix_kernel(pt_ref, *refs, g):
    lf_refs, lfn_ref, tri_ref, o_ref, carry_ref = refs[:g], refs[g], refs[g + 1], refs[g + 2], refs[g + 3]

    @pl.when(pl.program_id(1) == 0)
    def _():
        carry_ref[...] = jnp.zeros_like(carry_ref)

    tri = tri_ref[...]
    xs = [lf_refs[gg][...] for gg in range(g)]
    x1, x2, x3 = _split3(jnp.concatenate(xs, axis=0))
    within = _dot(x1, tri) + _dot(x2, tri) + _dot(x3, tri)
    carry = carry_ref[:, 0:1] + lfn_ref[...]
    for gg in reversed(range(g)):
        excl = within[gg * H_F:(gg + 1) * H_F] + carry
        o_ref[gg] = excl
        carry = excl[:, 0:1] + xs[gg][:, 0:1]
    carry_ref[...] = jnp.broadcast_to(carry - lfn_ref[...], carry_ref.shape)


def _fox_suffix(page_table, lf_t, lf_new_cols, layer, g):
    bsz, n_pages = page_table.shape
    page = lf_t.shape[-1]
    n_steps = n_pages // g
    t = np.arange(page)
    tri = jnp.asarray(t[:, None] > t[None, :], BF16)

    def page_spec(gg):
        return pl.BlockSpec((None, None, H_F, page),
                            lambda b, s, pt: (layer, pt[b, (n_steps - 1 - s) * g + gg], 0, 0))

    grid_spec = pltpu.PrefetchScalarGridSpec(
        num_scalar_prefetch=1, grid=(bsz, n_steps),
        in_specs=[page_spec(gg) for gg in range(g)] +
                 [pl.BlockSpec((None, H_F, 1), lambda b, s, pt: (b, 0, 0)),
                  pl.BlockSpec((page, page), lambda b, s, pt: (0, 0))],
        out_specs=pl.BlockSpec((None, g, H_F, page), lambda b, s, pt: (b, n_steps - 1 - s, 0, 0)),
        scratch_shapes=[pltpu.VMEM((H_F, LANES), F32)])
    return pl.pallas_call(
        functools.partial(_fox_suffix_kernel, g=g), grid_spec=grid_spec,
        out_shape=jax.ShapeDtypeStruct((bsz, n_pages, H_F, page), F32),
        compiler_params=_cparams("arbitrary", "arbitrary"), name="fox_suffix",
    )(page_table, *([lf_t] * g), lf_new_cols, tri)


def _fox_decode_kernel(pt_ref, *refs, g):
    q_ref, k_refs, v_refs = refs[0], refs[1:1 + g], refs[1 + g:1 + 2 * g]
    bias_ref, kn_ref, vn_ref, o_ref, qb_ref, m_ref, l_ref, acc_ref = refs[1 + 2 * g:]
    step = pl.program_id(1)
    page = k_refs[0].shape[-1]

    @pl.when(step == 0)
    def _():
        qb_ref[...] = jnp.broadcast_to(q_ref[...], qb_ref.shape)
        m_ref[...] = jnp.full_like(m_ref, NEG_BIG)
        l_ref[...] = jnp.zeros_like(l_ref)
        acc_ref[...] = jnp.zeros_like(acc_ref)

    for h in range(H_F):
        qb = qb_ref[h]
        rows = [jnp.sum(k_refs[gg][h] * qb, axis=0, keepdims=True) + bias_ref[gg, h:h + 1, :] for gg in range(g)]
        s = jnp.concatenate(rows, axis=0)
        m_old = m_ref[h][:, 0:1]
        m_new = jnp.maximum(m_old, jnp.max(jnp.max(s, axis=1, keepdims=True), axis=0, keepdims=True))
        a = jnp.exp(m_old - m_new)
        p = jnp.exp(s - m_new)
        l_new = a * l_ref[h][:, 0:1] + jnp.sum(jnp.sum(p, axis=1, keepdims=True), axis=0, keepdims=True)
        acc = a * acc_ref[h]
        for gg in range(g):
            acc = acc + v_refs[gg][h] * p[gg:gg + 1, :]
        m_ref[h] = jnp.broadcast_to(m_new, (1, page))
        l_ref[h] = jnp.broadcast_to(l_new, (1, page))
        acc_ref[h] = acc

    @pl.when(step == pl.num_programs(1) - 1)
    def _():
        for h in range(H_F):
            m_run = m_ref[h][:, 0:1]
            s_new = jnp.sum(q_ref[h] * kn_ref[h], axis=0, keepdims=True)
            m_fin = jnp.maximum(m_run, s_new)
            a2 = jnp.exp(m_run - m_fin)
            p_new = jnp.exp(s_new - m_fin)
            num = a2 * jnp.sum(acc_ref[h], axis=1, keepdims=True) + p_new * vn_ref[h]
            o_ref[h] = num / (a2 * l_ref[h][:, 0:1] + p_new)


def _fox_decode(page_table, fqb, fk_new, fv_new, lf_new, cache_k, cache_v, cache_lf, layer):
    bsz, n_pages = page_table.shape
    page = cache_k.shape[2]
    g = _largest_divisor(n_pages, (8, 4, 2, 1))
    n_steps = n_pages // g
    k_t = jnp.transpose(cache_k, (0, 1, 3, 4, 2))
    v_t = jnp.transpose(cache_v, (0, 1, 3, 4, 2))
    lf_t = jnp.transpose(cache_lf, (0, 1, 3, 2))
    bias = _fox_suffix(page_table, lf_t, lf_new[:, :, None], layer, _largest_divisor(n_pages, (16, 8, 4, 2, 1)))
    col = lambda a: a.astype(F32).reshape(bsz, H_F, DH_F, 1)

    def page_spec(gg):
        return pl.BlockSpec((None, None, H_F, DH_F, page), lambda b, s, pt: (layer, pt[b, s * g + gg], 0, 0, 0))

    per_b = pl.BlockSpec((None, H_F, DH_F, 1), lambda b, s, pt: (b, 0, 0, 0))
    grid_spec = pltpu.PrefetchScalarGridSpec(
        num_scalar_prefetch=1, grid=(bsz, n_steps),
        in_specs=[per_b] + [page_spec(gg) for gg in range(g)] * 2 +
                 [pl.BlockSpec((None, g, H_F, page), lambda b, s, pt: (b, s, 0, 0)), per_b, per_b],
        out_specs=per_b,
        scratch_shapes=[pltpu.VMEM((H_F, DH_F, page), F32), pltpu.VMEM((H_F, 1, page), F32),
                        pltpu.VMEM((H_F, 1, page), F32), pltpu.VMEM((H_F, DH_F, page), F32)])
    out = pl.pallas_call(
        functools.partial(_fox_decode_kernel, g=g), grid_spec=grid_spec,
        out_shape=jax.ShapeDtypeStruct((bsz, H_F, DH_F, 1), F32),
        compiler_params=_cparams("arbitrary", "arbitrary"), name="fox_decode",
    )(page_table, col(fqb), *([k_t] * g), *([v_t] * g), bias, col(fk_new), col(fv_new))
    return out.reshape(bsz, H_F * DH_F)


def _mlstm_prompt_kernel(q_ref, k_ref, v_ref, cols_ref, ccols_ref, rows_ref, h_ref, cst_ref, mst_ref,
                         c_scr, m_scr, *, n_chunks, ln):
    i = pl.program_id(0)

    @pl.when(i == 0)
    def _():
        c_scr[...] = jnp.zeros_like(c_scr)
        m_scr[...] = jnp.zeros_like(m_scr)

    lane = lax.broadcasted_iota(I32, (ln, LANES), 1)
    one_col = jnp.where(lane == 0, 1.0, 0.0).astype(F32)
    tril = lax.broadcasted_iota(I32, (ln, ln), 1) <= lax.broadcasted_iota(I32, (ln, ln), 0)
    srow = lax.broadcasted_iota(I32, (LANES, 2 * LANES), 0)

    def chunk(c, _):
        t0 = pl.multiple_of(c * ln, ln)
        cols = cols_ref[pl.ds(t0, ln), :]
        ccols = ccols_ref[pl.ds(t0, ln), :]
        rows = rows_ref[c]
        for p in range(H_M // 2):
            q2 = q_ref[pl.ds(t0, ln), p * LANES:(p + 1) * LANES]
            k2 = k_ref[pl.ds(t0, ln), p * LANES:(p + 1) * LANES].astype(BF16)
            c_pair = c_scr[p]
            c_pair_b = c_pair.astype(BF16)
            new_pair = c_pair
            for hh in range(2):
                h = 2 * p + hh
                own = (lane < DK_M) if hh == 0 else (lane >= DK_M)
                qm = jnp.where(own, q2, 0.0).astype(BF16)
                v = v_ref[pl.ds(t0, ln), h * DV_M:(h + 1) * DV_M]
                v_aug = jnp.concatenate([v, one_col], axis=1)
                ig_col = cols[:, 8 + h:9 + h]
                bc_col = ccols[:, 12 + h:13 + h]
                ig_row = rows[8 + h:9 + h, :]
                bc_row = rows[28 + h:29 + h, :]
                m0 = m_scr[h][0:1, 0:1]
                d = jnp.where(tril, bc_col - bc_row + ig_row, -jnp.inf)
                inter = bc_col + m0
                m_row = jnp.maximum(jnp.max(d, axis=1, keepdims=True), inter)
                s_in = jnp.exp(inter - m_row)
                sq = _dot_nt(qm, k2) * jnp.exp(d - m_row)
                num_aug = _dot(sq.astype(BF16), v_aug.astype(BF16)) + s_in * _dot(qm, c_pair_b)
                den = num_aug[:, DV_M:DV_M + 1]
                hv = num_aug[:, :DV_M] / jnp.maximum(jnp.abs(den), jnp.exp(-m_row))
                h_ref[pl.ds(t0, ln), h * DV_M:(h + 1) * DV_M] = hv
                b_last = bc_col[ln - 1:ln, :]
                dec = b_last - bc_col + ig_col
                m_new = jnp.maximum(b_last + m0, jnp.max(dec, axis=0, keepdims=True))
                ws = jnp.exp(dec - m_new)
                s0 = jnp.exp(b_last + m0 - m_new)
                upd = s0 * c_pair + _dot_tn(k2, (ws * v_aug).astype(BF16))
                rows_own = (srow < DK_M) if hh == 0 else (srow >= DK_M)
                new_pair = jnp.where(rows_own, upd, new_pair)
                m_scr[h] = jnp.broadcast_to(m_new, (8, LANES))
            c_scr[p] = new_pair
        return 0

    lax.fori_loop(0, n_chunks, chunk, 0)
    cst_ref[...] = c_scr[...]
    mst_ref[...] = m_scr[...]


def _mlstm_prompt(mq, mk, mv, cols, ccols, rows, g):
    s = mq.shape[0]
    ln = M_CHUNK
    tb = g * ln
    rows_ch = rows.reshape(32, s // ln, ln).transpose(1, 0, 2)
    rowb = lambda n: pl.BlockSpec((tb, n), lambda i: (i, 0))
    h, cst, mst = pl.pallas_call(
        functools.partial(_mlstm_prompt_kernel, n_chunks=g, ln=ln), grid=(s // tb,),
        in_specs=[rowb(256), rowb(256), rowb(512), rowb(LANES), rowb(LANES),
                  pl.BlockSpec((g, 32, ln), lambda i: (i, 0, 0))],
        out_specs=[rowb(512), pl.BlockSpec((2, LANES, 2 * LANES), lambda i: (0, 0, 0)),
                   pl.BlockSpec((H_M, 8, LANES), lambda i: (0, 0, 0))],
        out_shape=[jax.ShapeDtypeStruct((s, 512), F32), jax.ShapeDtypeStruct((2, LANES, 2 * LANES), F32),
                   jax.ShapeDtypeStruct((H_M, 8, LANES), F32)],
        scratch_shapes=[pltpu.VMEM((2, LANES, 2 * LANES), F32), pltpu.VMEM((H_M, 8, LANES), F32)],
        compiler_params=_cparams("arbitrary"), name="mlstm_prompt",
    )(mq, mk, mv, cols, ccols, rows_ch)
    c_new = cst[:, :, :DV_M].reshape(H_M, DK_M, DV_M)
    n_new = cst[:, :, DV_M].reshape(H_M, DK_M)
    m_new = mst[:, 0, 0]
    return h, c_new, n_new, m_new


def _mlstm_step_kernel(q_ref, k_ref, v_ref, cols_ref, c_ref, n_ref, m_ref, h_ref, cn_ref, nn_ref, mn_ref):
    cols = cols_ref[...]
    lane = lax.broadcasted_iota(I32, (1, LANES), 1)
    lane8 = lax.broadcasted_iota(I32, (8, LANES), 1)
    row8 = lax.broadcasted_iota(I32, (8, LANES), 0)
    srow = lax.broadcasted_iota(I32, (LANES, 1), 0)
    m_out = jnp.zeros((1, LANES), F32)
    for p in range(H_M // 2):
        q2 = q_ref[:, p * LANES:(p + 1) * LANES]
        k2 = k_ref[:, p * LANES:(p + 1) * LANES]
        q2r = q2.astype(BF16).astype(F32)
        k2r = k2.astype(BF16).astype(F32)
        n_pair = n_ref[:, p * LANES:(p + 1) * LANES]
        c_pair = c_ref[p * LANES:(p + 1) * LANES, :]
        c_pair_b = c_pair.astype(BF16)
        upd = jnp.zeros((LANES, LANES), F32)
        s0_rows = jnp.zeros((LANES, 1), F32)
        n_new = jnp.zeros((1, LANES), F32)
        for hh in range(2):
            h = 2 * p + hh
            own = (lane < DK_M) if hh == 0 else (lane >= DK_M)
            own8 = (lane8 < DK_M) if hh == 0 else (lane8 >= DK_M)
            qf = jnp.where(own, q2r, 0.0)
            v = v_ref[:, h * DV_M:(h + 1) * DV_M]
            ig = cols[:, 8 + h:9 + h]
            lf = cols[:, 12 + h:13 + h]
            m0 = m_ref[:, h:h + 1]
            inter = lf + m0
            m_row = jnp.maximum(ig, inter)
            s_in = jnp.exp(inter - m_row)
            wqk = jnp.sum(qf * k2r, axis=1, keepdims=True) * jnp.exp(ig - m_row)
            q_c = _dot(jnp.broadcast_to(qf, (8, LANES)).astype(BF16), c_pair_b)[0:1, :]
            q_n = jnp.sum(qf * n_pair.astype(BF16).astype(F32), axis=1, keepdims=True)
            num = wqk * v.astype(BF16).astype(F32) + s_in * q_c
            den = wqk + s_in * q_n
            h_ref[:, h * DV_M:(h + 1) * DV_M] = num / jnp.maximum(jnp.abs(den), jnp.exp(-m_row))
            ws = jnp.exp(ig - m_row)
            km8 = jnp.where(row8 == 0, jnp.where(own8, jnp.broadcast_to(k2r, (8, LANES)), 0.0), 0.0).astype(BF16)
            wv8 = jnp.where(row8 == 0, jnp.broadcast_to(ws * v, (8, LANES)), 0.0).astype(BF16)
            upd = upd + _dot_tn(km8, wv8)
            s0_rows = jnp.where((srow < DK_M) if hh == 0 else (srow >= DK_M), s_in, s0_rows)
            n_new = jnp.where(own, s_in * n_pair + ws * k2, n_new)
            m_out = jnp.where(lane == h, m_row, m_out)
        cn_ref[p * LANES:(p + 1) * LANES, :] = s0_rows * c_pair + upd
        nn_ref[:, p * LANES:(p + 1) * LANES] = n_new
    mn_ref[...] = m_out


def _mlstm_step(mq, mk, mv, cols, c0, n0, m0):
    bsz = mq.shape[0]
    b3 = lambda r, n: pl.BlockSpec((None, r, n), lambda i: (i, 0, 0))
    h, cn, nn, mn = pl.pallas_call(
        _mlstm_step_kernel, grid=(bsz,),
        in_specs=[b3(1, 256), b3(1, 256), b3(1, 512), b3(1, LANES), b3(H_M * DK_M, DV_M), b3(1, 256), b3(1, H_M)],
        out_specs=[b3(1, 512), b3(H_M * DK_M, DV_M), b3(1, 256), b3(1, LANES)],
        out_shape=[jax.ShapeDtypeStruct((bsz, 1, 512), F32), jax.ShapeDtypeStruct((bsz, H_M * DK_M, DV_M), F32),
                   jax.ShapeDtypeStruct((bsz, 1, 256), F32), jax.ShapeDtypeStruct((bsz, 1, LANES), F32)],
        compiler_params=_cparams("arbitrary"), name="mlstm_step",
    )(mq[:, None, :], mk[:, None, :], mv[:, None, :], cols[:, None, :],
      c0.reshape(bsz, H_M * DK_M, DV_M), n0.reshape(bsz, 1, H_M * DK_M), m0.reshape(bsz, 1, H_M))
    return (h[:, 0, :], cn.reshape(bsz, H_M, DK_M, DV_M), nn.reshape(bsz, H_M, DK_M), mn[:, 0, :H_M])


def _merge_kernel(x_ref, cy_ref, fy_ref, mh_ref, mo_ref, wg_ref, bg_ref, gm_ref, wb_ref, wo_ref, g_ref, b_ref, o_ref):
    x = x_ref[...]
    gates = jax.nn.sigmoid(_dot(x.astype(BF16), wg_ref[...]) + bg_ref[...])
    segs = []
    for h in range(H_M):
        seg = mh_ref[:, h * DV_M:(h + 1) * DV_M]
        segs.append(seg * lax.rsqrt(jnp.mean(seg * seg, axis=-1, keepdims=True) + LN_EPS))
    my = (jnp.concatenate(segs, axis=1) * gm_ref[...] * mo_ref[...]).astype(BF16)
    mix = gates[:, :D_MODEL] * _dot(cy_ref[...], wb_ref[0])
    mix = mix + gates[:, D_MODEL:2 * D_MODEL] * _dot(fy_ref[...], wb_ref[1])
    mix = mix + gates[:, 2 * D_MODEL:] * _dot(my, wb_ref[2])
    y = _dot(mix.astype(BF16), wo_ref[...])
    o_ref[...] = _layer_norm(ALPHA * x + y, g_ref[...], b_ref[...])


def _merge(x, conv_y, fox_y, m_h, m_o, w, tm):
    r = x.shape[0]
    rowb = lambda n: pl.BlockSpec((tm, n), lambda i: (i, 0))
    full = lambda a: pl.BlockSpec(a.shape, lambda i: (0,) * a.ndim)
    consts = [w['wg'], w['bg'], w['g_mnorm'], w['w_branch'], w['w_mix_out'], w['g_ln1'], w['b_ln1']]
    return pl.pallas_call(
        _merge_kernel, grid=(r // tm,),
        in_specs=[rowb(D_MODEL), rowb(512), rowb(512), rowb(512), rowb(512)] + [full(a) for a in consts],
        out_specs=rowb(D_MODEL), out_shape=jax.ShapeDtypeStruct((r, D_MODEL), F32),
        compiler_params=_cparams("arbitrary"), name="merge",
    )(x, conv_y, fox_y, m_h, m_o, *consts)


def _memkv_kernel(mem_ref, wk_ref, wv_ref, k_ref, v_ref, kb_ref, vb_ref):
    mb = mem_ref[...].astype(BF16)
    k = _dot(mb, wk_ref[...])
    v = _dot(mb, wv_ref[...])
    k_ref[...] = k
    v_ref[...] = v
    kb_ref[...] = k.astype(BF16)
    vb_ref[...] = v.astype(BF16)


def _memkv(mem, wk, wv):
    n = mem.shape[0]
    full = lambda a: pl.BlockSpec(a.shape, lambda i: (0,) * a.ndim)
    sd = lambda dt: jax.ShapeDtypeStruct((n, D_MODEL), dt)
    return pl.pallas_call(
        _memkv_kernel, grid=(1,),
        in_specs=[full(mem), full(wk), full(wv)],
        out_specs=[pl.BlockSpec((n, D_MODEL), lambda i: (0, 0))] * 4,
        out_shape=[sd(F32), sd(F32), sd(BF16), sd(BF16)],
        compiler_params=_cparams("arbitrary"), name="memkv",
    )(mem, wk, wv)


def _xattn_prompt_kernel(x_ref, kb_ref, vb_ref, wq_ref, wo_ref, g_ref, b_ref, o_ref):
    x = x_ref[...]
    q = (_dot(x.astype(BF16), wq_ref[...]) * DH_X ** -0.5).astype(BF16)
    outs = []
    for h in range(H_X):
        sl = slice(h * DH_X, (h + 1) * DH_X)
        s = _dot_nt(q[:, sl], kb_ref[:, sl])
        s = s - jnp.max(s, axis=1, keepdims=True)
        p = jnp.exp(s)
        o = _dot(p.astype(BF16), vb_ref[:, sl]) / jnp.sum(p, axis=1, keepdims=True)
        outs.append(o.astype(BF16))
    y = _dot(jnp.concatenate(outs, axis=1), wo_ref[...])
    o_ref[...] = _layer_norm(ALPHA * x + y, g_ref[...], b_ref[...])


def _xattn_prompt(x, kb, vb, w, tm):
    r = x.shape[0]
    rowb = pl.BlockSpec((tm, D_MODEL), lambda i: (i, 0))
    full = lambda a: pl.BlockSpec(a.shape, lambda i: (0,) * a.ndim)
    consts = [kb, vb, w['w_xq'], w['w_xo'], w['g_ln2'], w['b_ln2']]
    return pl.pallas_call(
        _xattn_prompt_kernel, grid=(r // tm,),
        in_specs=[rowb] + [full(a) for a in consts],
        out_specs=rowb, out_shape=jax.ShapeDtypeStruct((r, D_MODEL), F32),
        compiler_params=_cparams("arbitrary"), name="xattn_prompt",
    )(x, *consts)


def _linear_kernel(x_ref, w_ref, o_ref, *, scale):
    o_ref[...] = (_dot(x_ref[...].astype(BF16), w_ref[...]) * scale).astype(o_ref.dtype)


def _linear(x, w, scale, out_dtype):
    r, n = x.shape[0], w.shape[1]
    full = lambda a: pl.BlockSpec(a.shape, lambda i: (0,) * a.ndim)
    return pl.pallas_call(
        functools.partial(_linear_kernel, scale=scale), grid=(1,),
        in_specs=[full(x), full(w)], out_specs=pl.BlockSpec((r, n), lambda i: (0, 0)),
        out_shape=jax.ShapeDtypeStruct((r, n), out_dtype),
        compiler_params=_cparams("arbitrary"), name="linear",
    )(x, w)


def _linear_ln_kernel(x_ref, a_ref, w_ref, g_ref, b_ref, o_ref):
    y = _dot(a_ref[...].astype(BF16), w_ref[...])
    o_ref[...] = _layer_norm(ALPHA * x_ref[...] + y, g_ref[...], b_ref[...])


def _linear_ln(x, a, w, g, b):
    full = lambda t: pl.BlockSpec(t.shape, lambda i: (0,) * t.ndim)
    return pl.pallas_call(
        _linear_ln_kernel, grid=(1,),
        in_specs=[full(x), full(a), full(w), full(g), full(b)],
        out_specs=pl.BlockSpec(x.shape, lambda i: (0, 0)),
        out_shape=jax.ShapeDtypeStruct(x.shape, F32),
        compiler_params=_cparams("arbitrary"), name="linear_ln",
    )(x, a, w, g, b)


def _xattn_decode_kernel(q_ref, k_ref, v_ref, o_ref):
    w = q_ref.shape[-1]
    row = lax.broadcasted_iota(I32, (8, w), 0)
    lane = lax.broadcasted_iota(I32, (8, w), 1)
    own = (lane // DH_X) == row
    qrows = jnp.where(own, jnp.broadcast_to(q_ref[...].astype(F32), (8, w)), 0.0).astype(BF16)
    s = _dot_nt(qrows, k_ref[...].astype(BF16))
    s = s - jnp.max(s, axis=1, keepdims=True)
    p = jnp.exp(s)
    o = _dot(p.astype(BF16), v_ref[...].astype(BF16)) / jnp.sum(p, axis=1, keepdims=True)
    o_ref[...] = jnp.sum(jnp.where(own, o, 0.0), axis=0, keepdims=True)


def _xattn_decode(q, mem_k, mem_v):
    bsz, n_mem = mem_k.shape[0], mem_k.shape[1]
    b3 = lambda r: pl.BlockSpec((None, r, D_MODEL), lambda i: (i, 0, 0))
    out = pl.pallas_call(
        _xattn_decode_kernel, grid=(bsz,),
        in_specs=[b3(1), b3(n_mem), b3(n_mem)], out_specs=b3(1),
        out_shape=jax.ShapeDtypeStruct((bsz, 1, D_MODEL), F32),
        compiler_params=_cparams("arbitrary"), name="xattn_decode",
    )(q[:, None, :], mem_k.reshape(bsz, n_mem, D_MODEL), mem_v.reshape(bsz, n_mem, D_MODEL))
    return out[:, 0, :]


def _router_kernel(x_ref, w_ref, b_ref, lst_ref, ti_ref, tg_ref, cnt_ref, carry_ref):
    i = pl.program_id(0)

    @pl.when(i == 0)
    def _():
        carry_ref[...] = jnp.zeros_like(carry_ref)

    x1, x2, x3 = _split3(x_ref[...])
    w1, w2, w3 = w_ref[0], w_ref[1], w_ref[2]
    logits = (_dot(x1, w1) + (_dot(x1, w2) + _dot(x2, w1)) + (_dot(x1, w3) + _dot(x2, w2) + _dot(x3, w1))) + b_ref[...]
    tm = logits.shape[0]
    lane = lax.broadcasted_iota(I32, (tm, LANES), 1)
    lane_f = lane.astype(F32)
    vals = logits
    tops, idxs, sels = [], [], []
    for _ in range(TOP_K):
        mx = jnp.max(vals, axis=1, keepdims=True)
        idx = jnp.min(jnp.where(vals == mx, lane_f, float(LANES)), axis=1, keepdims=True)
        sel = lane_f == idx
        tops.append(mx)
        idxs.append(idx)
        sels.append(sel)
        vals = jnp.where(sel, -jnp.inf, vals)
    exps = [jnp.exp(t - tops[0]) for t in tops]
    den = exps[0] + exps[1] + exps[2] + exps[3]
    cnt = jnp.zeros((tm, LANES), F32)
    for sel in sels:
        cnt = cnt + jnp.where(sel, 1.0, 0.0)
    excl = _dot(lst_ref[...], cnt.astype(BF16)) + carry_ref[0:1, :]
    ti = jnp.zeros((tm, LANES), F32)
    tg = jnp.zeros((tm, LANES), F32)
    for k in range(TOP_K):
        rank = jnp.sum(jnp.where(sels[k], excl, 0.0), axis=1, keepdims=True)
        ti = jnp.where(lane == k, idxs[k], ti)
        ti = jnp.where(lane == TOP_K + k, rank, ti)
        tg = jnp.where(lane == k, exps[k] / den, tg)
    ti_ref[...] = ti.astype(I32)
    tg_ref[...] = tg
    carry_ref[...] = carry_ref[...] + jnp.sum(cnt, axis=0, keepdims=True)
    cnt_ref[...] = carry_ref[...]


def _router(x, w3, b, tm):
    r = x.shape[0]
    t = np.arange(tm)
    lst = jnp.asarray(t[:, None] > t[None, :], BF16)
    full = lambda a: pl.BlockSpec(a.shape, lambda i: (0,) * a.ndim)
    rowb = lambda n: pl.BlockSpec((tm, n), lambda i: (i, 0))
    return pl.pallas_call(
        _router_kernel, grid=(r // tm,),
        in_specs=[rowb(D_MODEL), full(w3), full(b), full(lst)],
        out_specs=[rowb(LANES), rowb(LANES), pl.BlockSpec((8, LANES), lambda i: (0, 0))],
        out_shape=[jax.ShapeDtypeStruct((r, LANES), I32), jax.ShapeDtypeStruct((r, LANES), F32),
                   jax.ShapeDtypeStruct((8, LANES), F32)],
        scratch_shapes=[pltpu.VMEM((8, LANES), F32)],
        compiler_params=_cparams("arbitrary"), name="router",
    )(x, w3, b, lst)


def _moe_dispatch_kernel(idx_hbm, x_ref, xs_in, xs_hbm, idx_smem, sem_idx, sem):
    del xs_in
    i = pl.program_id(0)
    tm = x_ref.shape[0]
    idx_cp = pltpu.make_async_copy(idx_hbm.at[i], idx_smem, sem_idx)
    idx_cp.start()
    idx_cp.wait()

    def send(r, _):
        for k in range(TOP_K):
            pltpu.make_async_copy(x_ref.at[pl.ds(r, 1)], xs_hbm.at[pl.ds(idx_smem[r * TOP_K + k], 1)], sem).start()
        return 0

    lax.fori_loop(0, tm, send, 0)
    for k in range(TOP_K):
        pltpu.make_async_copy(x_ref, xs_hbm.at[pl.ds(0, tm)], sem).wait()


def _moe_dispatch(idx, x, rows, tm):
    n = x.shape[0]
    return pl.pallas_call(
        _moe_dispatch_kernel, grid=(n // tm,),
        in_specs=[pl.BlockSpec(memory_space=pl.ANY), pl.BlockSpec((tm, D_MODEL), lambda i: (i, 0)),
                  pl.BlockSpec(memory_space=pl.ANY)],
        out_specs=pl.BlockSpec(memory_space=pl.ANY),
        out_shape=jax.ShapeDtypeStruct((rows, D_MODEL), F32),
        scratch_shapes=[pltpu.SMEM((tm * TOP_K,), I32), pltpu.SemaphoreType.DMA(()), pltpu.SemaphoreType.DMA(())],
        input_output_aliases={2: 0},
        compiler_params=_cparams("arbitrary"), name="moe_dispatch",
    )(idx, x, jnp.zeros((rows, D_MODEL), F32))


def _moe_ffn_kernel(be_ref, nv_ref, xs_ref, wgu_ref, bgu_ref, wd_ref, bd_ref, y_ref, wgu_b, wd_b):
    i = pl.program_id(0)

    @pl.when(i < nv_ref[0])
    def _():
        e = be_ref[i]
        e_prev = be_ref[jnp.maximum(i - 1, 0)]

        @pl.when((i == 0) | (e != e_prev))
        def _():
            wgu_b[...] = wgu_ref[...].astype(BF16)
            wd_b[...] = wd_ref[...].astype(BF16)

        gu = _dot(xs_ref[...].astype(BF16), wgu_b[...]) + bgu_ref[...]
        g = jnp.minimum(gu[:, :D_EXPERT], SWIGLU_LIMIT)
        u = jnp.clip(gu[:, D_EXPERT:], -SWIGLU_LIMIT, SWIGLU_LIMIT)
        act = (u + 1.0) * g * jax.nn.sigmoid(SWIGLU_ALPHA * g)
        y_ref[...] = _dot(act.astype(BF16), wd_b[...]) + bd_ref[...]

    @pl.when(i >= nv_ref[0])
    def _():
        y_ref[...] = jnp.zeros_like(y_ref)


def _moe_ffn(blk_e, n_valid, xs, w, l, tb):
    rows = xs.shape[0]
    wsel = lambda i, be, nv: (l, be[i], 0, 0)
    grid_spec = pltpu.PrefetchScalarGridSpec(
        num_scalar_prefetch=2, grid=(rows // tb,),
        in_specs=[pl.BlockSpec((tb, D_MODEL), lambda i, be, nv: (jnp.minimum(i, nv[0] - 1), 0)),
                  pl.BlockSpec((None, None, D_MODEL, 2 * D_EXPERT), wsel),
                  pl.BlockSpec((None, None, 1, 2 * D_EXPERT), wsel),
                  pl.BlockSpec((None, None, D_EXPERT, D_MODEL), wsel),
                  pl.BlockSpec((None, None, 1, D_MODEL), wsel)],
        out_specs=pl.BlockSpec((tb, D_MODEL), lambda i, be, nv: (i, 0)),
        scratch_shapes=[pltpu.VMEM((D_MODEL, 2 * D_EXPERT), BF16), pltpu.VMEM((D_EXPERT, D_MODEL), BF16)])
    return pl.pallas_call(
        _moe_ffn_kernel, grid_spec=grid_spec,
        out_shape=jax.ShapeDtypeStruct((rows, D_MODEL), F32),
        compiler_params=_cparams("arbitrary"), name="moe_ffn",
    )(blk_e, n_valid, xs, w['w_gate_up'], w['b_gate_up'][:, :, None, :], w['w_down'], w['b_down'][:, :, None, :])


def _moe_combine_kernel(idx_hbm, x_ref, tg_ref, ys_hbm, g_ref, b_ref, o_ref, idx_smem, ybuf, sem_idx, sem):
    i = pl.program_id(0)
    tm = x_ref.shape[0]
    idx_cp = pltpu.make_async_copy(idx_hbm.at[i], idx_smem, sem_idx)
    idx_cp.start()
    idx_cp.wait()

    def fetch(r, _):
        for k in range(TOP_K):
            pltpu.make_async_copy(ys_hbm.at[pl.ds(idx_smem[r * TOP_K + k], 1)], ybuf.at[k, pl.ds(r, 1)], sem).start()
        return 0

    lax.fori_loop(0, tm, fetch, 0)
    for k in range(TOP_K):
        pltpu.make_async_copy(ys_hbm.at[pl.ds(0, tm)], ybuf.at[k], sem).wait()
    tg = tg_ref[...]
    ff = tg[:, 0:1] * ybuf[0]
    for k in range(1, TOP_K):
        ff = ff + tg[:, k:k + 1] * ybuf[k]
    o_ref[...] = _layer_norm(ALPHA * x_ref[...] + ff, g_ref[...], b_ref[...])


def _moe_combine(idx, x, tg, ys, g, b, tm):
    n = x.shape[0]
    rowb = lambda wd: pl.BlockSpec((tm, wd), lambda i: (i, 0))
    full = lambda t: pl.BlockSpec(t.shape, lambda i: (0,) * t.ndim)
    return pl.pallas_call(
        _moe_combine_kernel, grid=(n // tm,),
        in_specs=[pl.BlockSpec(memory_space=pl.ANY), rowb(D_MODEL), rowb(LANES), pl.BlockSpec(memory_space=pl.ANY),
                  full(g), full(b)],
        out_specs=rowb(D_MODEL), out_shape=jax.ShapeDtypeStruct((n, D_MODEL), F32),
        scratch_shapes=[pltpu.SMEM((tm * TOP_K,), I32), pltpu.VMEM((TOP_K, tm, D_MODEL), F32),
                        pltpu.SemaphoreType.DMA(()), pltpu.SemaphoreType.DMA(())],
        compiler_params=_cparams("arbitrary"), name="moe_combine",
    )(idx, x, tg, ys, g, b)


def _moe(x, w, l, tb, tm):
    n = x.shape[0]
    a = n * TOP_K
    ti, tg, cnt = _router(x, w['w_router3'], w['b_router'], tm)
    top_e = ti[:, :TOP_K]
    rank = ti[:, TOP_K:2 * TOP_K]
    counts = cnt[0, :N_EXPERTS].astype(I32)
    n_blk = -(-a // tb) + N_EXPERTS
    padded = (counts + tb - 1) // tb * tb
    pend = jnp.cumsum(padded)
    dest = (pend - padded)[top_e] + rank
    idx = dest.reshape(n // tm, tm * TOP_K)
    n_valid = (pend[-1:] // tb).astype(I32)
    blk_id = jnp.minimum(jnp.arange(n_blk, dtype=I32), n_valid - 1)
    blk_e = jnp.minimum(jnp.sum((pend[None, :] <= (blk_id * tb)[:, None]).astype(I32), axis=1), N_EXPERTS - 1)
    xs = _moe_dispatch(idx, x, n_blk * tb, tm)
    ys = _moe_ffn(blk_e, n_valid, xs, w, l, tb)
    return _moe_combine(idx, x, tg, ys, w['g_ln3'], w['b_ln3'], tm)


def _layer_weights(l, w_in, b_in, b_fox_f, b_mlstm_f, g_mlstm_norm, w_branch, w_mix_out, g_ln1, b_ln1,
                   w_xq, w_xk, w_xv, w_xo, g_ln2, b_ln2, w_router, b_router, w_gate_up, b_gate_up,
                   w_down, b_down, g_ln3, b_ln3):
    w = _split_w_in(w_in[l], b_in[l], b_fox_f[l], b_mlstm_f[l])
    wr = jnp.pad(w_router[l], ((0, 0), (0, LANES - N_EXPERTS)))
    r1 = wr.astype(BF16)
    r2 = (wr - r1.astype(F32)).astype(BF16)
    r3 = (wr - r1.astype(F32) - r2.astype(F32)).astype(BF16)
    w.update(
        g_mnorm=g_mlstm_norm[l][None], w_branch=w_branch[l].astype(BF16), w_mix_out=w_mix_out[l].astype(BF16),
        g_ln1=g_ln1[l][None], b_ln1=b_ln1[l][None],
        w_xq=w_xq[l].astype(BF16), w_xk=w_xk[l].astype(BF16), w_xv=w_xv[l].astype(BF16), w_xo=w_xo[l].astype(BF16),
        g_ln2=g_ln2[l][None], b_ln2=b_ln2[l][None],
        w_router3=jnp.stack([r1, r2, r3]),
        b_router=jnp.pad(b_router[l], (0, LANES - N_EXPERTS), constant_values=NEG_BIG)[None],
        w_gate_up=w_gate_up, b_gate_up=b_gate_up, w_down=w_down, b_down=b_down,
        g_ln3=g_ln3[l][None], b_ln3=b_ln3[l][None])
    return w


def kernel(x_prompt, x_sample, mem_prompt, cache_fox_k, cache_fox_v, cache_fox_lf, page_table, state_conv,
           state_mlstm_c, state_mlstm_n, state_mlstm_m, cache_mem_k, cache_mem_v, w_in, b_in, b_fox_f, b_mlstm_f,
           w_dw, b_dw, g_conv_ln, b_conv_ln, g_mlstm_norm, w_branch, w_mix_out, g_ln1, b_ln1, w_xq, w_xk, w_xv,
           w_xo, g_ln2, b_ln2, w_router, b_router, w_gate_up, b_gate_up, w_down, b_down, g_ln3, b_ln3):
    b_p, s, _ = x_prompt.shape
    b_s = x_sample.shape[0]
    assert b_p == 1 and x_sample.shape[1] == 1
    depth = w_in.shape[0]
    xp = x_prompt.reshape(s, D_MODEL)
    xs = x_sample.reshape(b_s, D_MODEL)
    mem = mem_prompt.reshape(-1, D_MODEL)
    tm = _row_block(s, 256)
    tq = _row_block(s, 256)
    tk = _row_block(s, 512)
    g_chunks = 8 if s % (8 * M_CHUNK) == 0 else 1
    tb_p = 256 if s * TOP_K >= 256 * N_EXPERTS else 8
    rows_p, rows_s = [], []
    for l in range(depth):
        w = _layer_weights(l, w_in, b_in, b_fox_f, b_mlstm_f, g_mlstm_norm, w_branch, w_mix_out, g_ln1, b_ln1,
                           w_xq, w_xk, w_xv, w_xo, g_ln2, b_ln2, w_router, b_router, w_gate_up, b_gate_up,
                           w_down, b_down, g_ln3, b_ln3)
        pr = _inproj(xp, w, tm)
        conv_y = _conv_prompt(pr['u'], w_dw[l], b_dw[l], g_conv_ln[l], b_conv_ln[l], tm)
        fox_y = _fox_prompt(pr['fqb'], pr['fkb'], pr['fvb'], pr['rows'][16:24], tq, tk)
        m_h, m_c, m_n, m_m = _mlstm_prompt(pr['mq'], pr['mk'], pr['mv'], pr['cols'], pr['ccols'], pr['rows'], g_chunks)
        xp = _merge(xp, conv_y, fox_y, m_h, pr['mo'], w, tm)
        mem_k, mem_v, mem_kb, mem_vb = _memkv(mem, w['w_xk'], w['w_xv'])
        xp = _xattn_prompt(xp, mem_kb, mem_vb, w, tm)
        xp = _moe(xp, w, l, tb_p, tm)
        rows_p.append((pr['fk'].reshape(1, s, H_F, DH_F), pr['fv'].reshape(1, s, H_F, DH_F),
                       pr['cols'][:, :H_F].reshape(1, s, H_F), pr['u'][s - (K_CONV - 1):][None],
                       m_c[None], m_n[None], m_m[None],
                       mem_k.reshape(1, -1, H_X, DH_X), mem_v.reshape(1, -1, H_X, DH_X)))
        sr = _inproj(xs, w, b_s)
        conv_y, conv_st = _conv_step(state_conv[l], sr['u'], w_dw[l], b_dw[l], g_conv_ln[l], b_conv_ln[l])
        lf_new = sr['cols'][:, :H_F]
        fox_y = _fox_decode(page_table, sr['fqb'], sr['fk'], sr['fv'], lf_new, cache_fox_k, cache_fox_v,
                            cache_fox_lf, l)
        m_h, m_c, m_n, m_m = _mlstm_step(sr['mq'], sr['mk'], sr['mv'], sr['cols'], state_mlstm_c[l],
                                         state_mlstm_n[l], state_mlstm_m[l])
        xs = _merge(xs, conv_y, fox_y.astype(BF16), m_h, sr['mo'], w, b_s)
        q = _linear(xs, w['w_xq'], DH_X ** -0.5, BF16)
        att = _xattn_decode(q, cache_mem_k[l], cache_mem_v[l])
        xs = _linear_ln(xs, att, w['w_xo'], w['g_ln2'], w['b_ln2'])
        xs = _moe(xs, w, l, 8, b_s)
        rows_s.append((sr['fk'].reshape(b_s, 1, H_F, DH_F), sr['fv'].reshape(b_s, 1, H_F, DH_F),
                       lf_new.reshape(b_s, 1, H_F), conv_st, m_c, m_n, m_m))
    outs_p = [jnp.stack(a) for a in zip(*rows_p)]
    outs_s = [jnp.stack(a) for a in zip(*rows_s)]
    return (xp.reshape(1, s, D_MODEL), xs.reshape(b_s, 1, D_MODEL), *outs_p, *outs_s)
```

```python
import functools

import numpy as np
import jax
import jax.numpy as jnp
from jax import lax
from jax.experimental import pallas as pl
from jax.experimental.pallas import tpu as pltpu

F32 = jnp.float32
BF16 = jnp.bfloat16
I32 = jnp.int32

D_MODEL = 1024
W_BR = 512
K_CONV = 31
H_F = 8
DH_F = 64
H_M = 4
DK_M = 64
DV_M = 128
M_CHUNK = 64
I_CAP = 15.0
H_X = 4
DH_X = 256
N_EXPERTS = 32
TOP_K = 4
D_EXPERT = 1024
SWIGLU_ALPHA = 1.702
SWIGLU_LIMIT = 7.0
LN_EPS = 1e-5
DEPTH = 2
ALPHA = (2.0 * DEPTH) ** 0.25
NEG_BIG = -1e30
LANES = 128
VMEM_LIMIT = 56 * 1024 * 1024


def _cparams(*sem):
    return pltpu.CompilerParams(dimension_semantics=sem, vmem_limit_bytes=VMEM_LIMIT)


def _dot(a, b):
    return jnp.dot(a, b, preferred_element_type=F32)


def _dot_nt(a, b):
    return lax.dot_general(a, b, (((1,), (1,)), ((), ())), preferred_element_type=F32)


def _dot_tn(a, b):
    return lax.dot_general(a, b, (((0,), (0,)), ((), ())), preferred_element_type=F32)


def _split3(a):
    a1 = a.astype(BF16)
    r = a - a1.astype(F32)
    a2 = r.astype(BF16)
    r = r - a2.astype(F32)
    return a1, a2, r.astype(BF16)


def _log_sigmoid(x):
    return jnp.minimum(x, 0.0) - jnp.log1p(jnp.exp(-jnp.abs(x)))


def _layer_norm(x, g, b):
    mu = jnp.mean(x, axis=-1, keepdims=True)
    xc = x - mu
    var = jnp.mean(xc * xc, axis=-1, keepdims=True)
    return xc * lax.rsqrt(var + LN_EPS) * g + b


def _row_block(n, pref):
    return pref if n % pref == 0 else n


def _gate_act(z, idx):
    ls = _log_sigmoid(z)
    tg = I_CAP * jnp.tanh(z * (1.0 / I_CAP))
    return jnp.where(idx < 8, ls, jnp.where(idx < 12, tg, jnp.where(idx < 16, ls, 0.0)))


def _inproj_kernel(x_ref, wc_ref, bc_ref, wf_ref, bf_ref, wm_ref, bm_ref, ws_ref, bs_ref, wst_ref, bst_ref,
                   ufull_ref, ublk_ref, lblk_ref,
                   u_ref, fqb_ref, fk_ref, fv_ref, fkb_ref, fvb_ref, mq_ref, mk_ref, mv_ref, mo_ref,
                   cols_ref, ccols_ref, rows_ref, carry_ref):
    i = pl.program_id(0)

    @pl.when(i == 0)
    def _():
        carry_ref[...] = jnp.zeros_like(carry_ref)

    tm = x_ref.shape[0]
    xb = x_ref[...].astype(BF16)
    zc = _dot(xb, wc_ref[...]) + bc_ref[...]
    u_ref[...] = zc[:, :W_BR] * jax.nn.sigmoid(zc[:, W_BR:])

    zf = _dot(xb, wf_ref[...]) + bf_ref[...]
    fk = zf[:, W_BR:2 * W_BR]
    fv = zf[:, 2 * W_BR:]
    fqb_ref[...] = (zf[:, :W_BR] * DH_F ** -0.5).astype(BF16)
    fk_ref[...] = fk
    fv_ref[...] = fv
    fkb_ref[...] = fk.astype(BF16)
    fvb_ref[...] = fv.astype(BF16)

    zm = _dot(xb, wm_ref[...]) + bm_ref[...]
    mq_ref[...] = zm[:, :256]
    mk_ref[...] = zm[:, 256:512] * DK_M ** -0.5
    mv_ref[...] = zm[:, 512:1024]
    mo_ref[...] = jax.nn.sigmoid(zm[:, 1024:])

    zs = _dot(xb, ws_ref[...]) + bs_ref[...]
    lane = lax.broadcasted_iota(I32, zs.shape, 1)
    act_c = _gate_act(zs, lane)
    cols_ref[...] = act_c
    c1, c2, c3 = _split3(act_c)
    lb = lblk_ref[...]
    ccols_ref[...] = _dot(lb, c1) + _dot(lb, c2) + _dot(lb, c3)

    zt = _dot_nt(wst_ref[...], xb) + bst_ref[...]
    row = lax.broadcasted_iota(I32, zt.shape, 0)
    act_r = _gate_act(zt, row)
    r1, r2, r3 = _split3(act_r)
    uf = ufull_ref[...]
    ub = ublk_ref[...]
    cum_full = _dot(r1, uf) + _dot(r2, uf) + _dot(r3, uf) + carry_ref[:, 0:1]
    cum_blk = _dot(r1, ub) + _dot(r2, ub) + _dot(r3, ub)
    carry_ref[...] = jnp.broadcast_to(cum_full[:, tm - 1:tm], carry_ref.shape)
    rows_ref[0:16, :] = act_r
    rows_ref[16:32, :] = jnp.where(row < 8, cum_full, cum_blk)


def _tri_consts(tm, chunk):
    t = np.arange(tm)
    upper = (t[:, None] <= t[None, :])
    same = (t[:, None] // chunk) == (t[None, :] // chunk)
    ufull = jnp.asarray(upper, BF16)
    ublk = jnp.asarray(upper & same, BF16)
    lblk = jnp.asarray(upper.T & same, BF16)
    return ufull, ublk, lblk


def _split_w_in(w_in, b_in, b_fox_f, b_mlstm_f):
    o = np.cumsum([0, 512, 512, 512, 512, 512, 8, 256, 256, 512, 4, 4, 512, 3072])
    sl = lambda a, i, j: a[..., o[i]:o[j]]
    wc, bc = sl(w_in, 0, 2), sl(b_in, 0, 2)
    wf, bf = sl(w_in, 2, 5), sl(b_in, 2, 5)
    wm = jnp.concatenate([sl(w_in, 6, 9), sl(w_in, 11, 12)], axis=-1)
    bm = jnp.concatenate([sl(b_in, 6, 9), sl(b_in, 11, 12)], axis=-1)
    wsm = jnp.concatenate([sl(w_in, 5, 6), sl(w_in, 9, 11)], axis=-1)
    bsm = jnp.concatenate([sl(b_in, 5, 6) + b_fox_f, sl(b_in, 9, 10), sl(b_in, 10, 11) + b_mlstm_f], axis=-1)
    ws = jnp.pad(wsm, ((0, 0), (0, LANES - 16)))
    bs = jnp.pad(bsm, (0, LANES - 16))
    wg, bg = sl(w_in, 12, 13), sl(b_in, 12, 13)
    return dict(wc=wc.astype(BF16), bc=bc[None], wf=wf.astype(BF16), bf=bf[None], wm=wm.astype(BF16), bm=bm[None],
                ws=ws.astype(BF16), bs=bs[None], wst=wsm.T.astype(BF16), bst=bsm[:, None],
                wg=wg.astype(BF16), bg=bg[None])


def _inproj(x, w, tm):
    r = x.shape[0]
    chunk = min(M_CHUNK, tm)
    ufull, ublk, lblk = _tri_consts(tm, chunk)
    rowb = lambda n: pl.BlockSpec((tm, n), lambda i: (i, 0))
    full = lambda a: pl.BlockSpec(a.shape, lambda i: (0,) * a.ndim)
    consts = [w['wc'], w['bc'], w['wf'], w['bf'], w['wm'], w['bm'], w['ws'], w['bs'], w['wst'], w['bst'],
              ufull, ublk, lblk]
    out_shape = [jax.ShapeDtypeStruct((r, 512), F32), jax.ShapeDtypeStruct((r, 512), BF16),
                 jax.ShapeDtypeStruct((r, 512), F32), jax.ShapeDtypeStruct((r, 512), F32),
                 jax.ShapeDtypeStruct((r, 512), BF16), jax.ShapeDtypeStruct((r, 512), BF16),
                 jax.ShapeDtypeStruct((r, 256), F32), jax.ShapeDtypeStruct((r, 256), F32),
                 jax.ShapeDtypeStruct((r, 512), F32), jax.ShapeDtypeStruct((r, 512), F32),
                 jax.ShapeDtypeStruct((r, LANES), F32), jax.ShapeDtypeStruct((r, LANES), F32),
                 jax.ShapeDtypeStruct((32, r), F32)]
    out_specs = [rowb(512)] * 6 + [rowb(256)] * 2 + [rowb(512)] * 2 + [rowb(LANES)] * 2 + \
                [pl.BlockSpec((32, tm), lambda i: (0, i))]
    outs = pl.pallas_call(
        _inproj_kernel, grid=(r // tm,),
        in_specs=[rowb(D_MODEL)] + [full(a) for a in consts],
        out_specs=out_specs, out_shape=out_shape,
        scratch_shapes=[pltpu.VMEM((16, LANES), F32)],
        compiler_params=_cparams("arbitrary"), name="inproj",
    )(x, *consts)
    keys = ['u', 'fqb', 'fk', 'fv', 'fkb', 'fvb', 'mq', 'mk', 'mv', 'mo', 'cols', 'ccols', 'rows']
    return dict(zip(keys, outs))


HALO = 32


def _conv_kernel(u_ref, w_ref, b_ref, g_ref, bln_ref, y_ref, buf_ref):
    i = pl.program_id(0)
    tm = u_ref.shape[0]

    @pl.when(i == 0)
    def _():
        buf_ref[0:HALO, :] = jnp.zeros((HALO, W_BR), F32)

    buf_ref[HALO:HALO + tm, :] = u_ref[...]
    acc = jnp.zeros((tm, W_BR), F32) + b_ref[...]
    off = HALO - (K_CONV - 1)
    for k in range(K_CONV):
        acc = acc + w_ref[k:k + 1, :] * buf_ref[off + k:off + k + tm, :]
    yn = _layer_norm(acc, g_ref[...], bln_ref[...])
    y_ref[...] = (yn * jax.nn.sigmoid(yn)).astype(y_ref.dtype)
    buf_ref[0:HALO, :] = buf_ref[tm:tm + HALO, :]


def _conv_prompt(u, w_dw, b_dw, g, b, tm):
    s = u.shape[0]
    full = lambda a: pl.BlockSpec(a.shape, lambda i: (0,) * a.ndim)
    args = [w_dw, b_dw[None], g[None], b[None]]
    return pl.pallas_call(
        _conv_kernel, grid=(s // tm,),
        in_specs=[pl.BlockSpec((tm, W_BR), lambda i: (i, 0))] + [full(a) for a in args],
        out_specs=pl.BlockSpec((tm, W_BR), lambda i: (i, 0)),
        out_shape=jax.ShapeDtypeStruct((s, W_BR), BF16),
        scratch_shapes=[pltpu.VMEM((HALO + tm, W_BR), F32)],
        compiler_params=_cparams("arbitrary"), name="conv_prompt",
    )(u, *args)


def _conv_step_kernel(st_ref, u_ref, w_ref, b_ref, g_ref, bln_ref, y_ref, nst_ref):
    st = st_ref[0]
    un = u_ref[0]
    acc = b_ref[...] + w_ref[K_CONV - 1:K_CONV, :] * un
    acc = acc + jnp.sum(w_ref[0:K_CONV - 1, :] * st, axis=0, keepdims=True)
    yn = _layer_norm(acc, g_ref[...], bln_ref[...])
    y_ref[0] = (yn * jax.nn.sigmoid(yn)).astype(y_ref.dtype)
    nst_ref[0, 0:K_CONV - 2, :] = st[1:, :]
    nst_ref[0, K_CONV - 2:K_CONV - 1, :] = un


def _conv_step(state, u, w_dw, b_dw, g, b):
    bsz = u.shape[0]
    full = lambda a: pl.BlockSpec(a.shape, lambda i: (0,) * a.ndim)
    args = [w_dw, b_dw[None], g[None], b[None]]
    y, nst = pl.pallas_call(
        _conv_step_kernel, grid=(bsz,),
        in_specs=[pl.BlockSpec((1, K_CONV - 1, W_BR), lambda i: (i, 0, 0)),
                  pl.BlockSpec((1, 1, W_BR), lambda i: (i, 0, 0))] + [full(a) for a in args],
        out_specs=[pl.BlockSpec((1, 1, W_BR), lambda i: (i, 0, 0)),
                   pl.BlockSpec((1, K_CONV - 1, W_BR), lambda i: (i, 0, 0))],
        out_shape=[jax.ShapeDtypeStruct((bsz, 1, W_BR), BF16),
                   jax.ShapeDtypeStruct((bsz, K_CONV - 1, W_BR), F32)],
        compiler_params=_cparams("arbitrary"), name="conv_step",
    )(state, u[:, None, :], *args)
    return y[:, 0, :], nst


def _fox_prompt_kernel(q_ref, k_ref, v_ref, c_ref, o_ref, s_a, s_b, p_a, p_b, m_s, l_s, a_s, acc_s, *, tq, tk):
    qi = pl.program_id(1)
    q2 = q_ref[...].astype(F32)
    lane = lax.broadcasted_iota(I32, (tq, LANES), 1)
    q_start = pl.multiple_of(qi * tq, tq)
    n_full = (qi * tq) // tk
    qms = [jnp.where(lane < DH_F, q2, 0.0).astype(BF16), jnp.where(lane < DH_F, 0.0, q2).astype(BF16)]
    c0s = [c_ref[hh:hh + 1, pl.ds(q_start, LANES)][:, 0:1] for hh in range(2)]

    def scores(j, s_ref):
        k_start = pl.multiple_of(j * tk, tk)
        kb = k_ref[pl.ds(k_start, tk), :]
        for hh in range(2):
            s_ref[hh] = _dot_nt(qms[hh], kb) + (c0s[hh] - c_ref[hh:hh + 1, pl.ds(k_start, tk)])

    def values(j, p_ref):
        k_start = pl.multiple_of(j * tk, tk)
        vb = v_ref[pl.ds(k_start, tk), :]
        for hh in range(2):
            acc_s[hh] = a_s[hh] * acc_s[hh] + _dot(p_ref[hh], vb)

    def softmax(j, s_ref, p_ref, masked):
        for hh in range(2):
            s = s_ref[hh]
            if masked:
                rows = q_start + lax.broadcasted_iota(I32, (tq, tk), 0)
                cols = j * tk + lax.broadcasted_iota(I32, (tq, tk), 1)
                s = jnp.where(cols <= rows, s, NEG_BIG)
            m_old = m_s[hh]
            m_new = jnp.maximum(m_old, jnp.max(s, axis=1, keepdims=True))
            a = jnp.exp(m_old - m_new)
            p = jnp.exp(s - m_new)
            p_ref[hh] = p.astype(BF16)
            m_s[hh] = m_new
            l_s[hh] = a * l_s[hh] + jnp.sum(p, axis=1, keepdims=True)
            a_s[hh] = a

    def stage(j, s_cur, s_nxt, p_cur, p_prev, last):
        if not last:
            scores(j + 1, s_nxt)
        values(jnp.maximum(j - 1, 0), p_prev)
        softmax(j, s_cur, p_cur, last)
        if last:
            values(j, p_cur)

    def by_parity(j, last):
        even = lax.rem(j, 2) == 0
        pl.when(even)(functools.partial(stage, j, s_a, s_b, p_a, p_b, last))
        pl.when(jnp.logical_not(even))(functools.partial(stage, j, s_b, s_a, p_b, p_a, last))

    p_b[...] = jnp.zeros_like(p_b)
    m_s[...] = jnp.full_like(m_s, NEG_BIG)
    l_s[...] = jnp.zeros_like(l_s)
    a_s[...] = jnp.ones_like(a_s)
    acc_s[...] = jnp.zeros_like(acc_s)
    scores(0, s_a)

    def body(j, _):
        by_parity(j, False)
        return 0

    lax.fori_loop(0, n_full, body, 0)
    by_parity(n_full, True)
    o_ref[...] = jnp.where(lane < DH_F, acc_s[0] / l_s[0], acc_s[1] / l_s[1]).astype(o_ref.dtype)


def _fox_prompt(fqb, fkb, fvb, crow, tq, tk):
    s = fqb.shape[0]
    c3 = crow.reshape(H_F // 2, 2, s)
    return pl.pallas_call(
        functools.partial(_fox_prompt_kernel, tq=tq, tk=tk), grid=(H_F // 2, s // tq),
        in_specs=[pl.BlockSpec((tq, LANES), lambda p, i: (i, p)),
                  pl.BlockSpec((s, LANES), lambda p, i: (0, p)),
                  pl.BlockSpec((s, LANES), lambda p, i: (0, p)),
                  pl.BlockSpec((None, 2, s), lambda p, i: (p, 0, 0))],
        out_specs=pl.BlockSpec((tq, LANES), lambda p, i: (i, p)),
        out_shape=jax.ShapeDtypeStruct((s, W_BR), BF16),
        scratch_shapes=[pltpu.VMEM((2, tq, tk), F32), pltpu.VMEM((2, tq, tk), F32),
                        pltpu.VMEM((2, tq, tk), BF16), pltpu.VMEM((2, tq, tk), BF16),
                        pltpu.VMEM((2, tq, 1), F32), pltpu.VMEM((2, tq, 1), F32), pltpu.VMEM((2, tq, 1), F32),
                        pltpu.VMEM((2, tq, LANES), F32)],
        compiler_params=_cparams("arbitrary", "arbitrary"), name="fox_prompt",
    )(fqb, fkb, fvb, c3)


def _largest_divisor(n, options):
    return next(o for o in options if n % o == 0)


def _fox_suffix_kernel(pt_ref, *refs, g):
    lf_refs, lfn_ref, tri_ref, o_ref, carry_ref = refs[:g], refs[g], refs[g + 1], refs[g + 2], refs[g + 3]

    @pl.when(pl.program_id(1) == 0)
    def _():
        carry_ref[...] = jnp.zeros_like(carry_ref)

    tri = tri_ref[...]
    xs = [lf_refs[gg][...] for gg in range(g)]
    x1, x2, x3 = _split3(jnp.concatenate(xs, axis=0))
    within = _dot(x1, tri) + _dot(x2, tri) + _dot(x3, tri)
    carry = carry_ref[:, 0:1] + lfn_ref[...]
    for gg in reversed(range(g)):
        excl = within[gg * H_F:(gg + 1) * H_F] + carry
        o_ref[gg] = excl
        carry = excl[:, 0:1] + xs[gg][:, 0:1]
    carry_ref[...] = jnp.broadcast_to(carry - lfn_ref[...], carry_ref.shape)


def _fox_suffix(page_table, lf_t, lf_new_cols, layer, g):
    bsz, n_pages = page_table.shape
    page = lf_t.shape[-1]
    n_steps = n_pages // g
    t = np.arange(page)
    tri = jnp.asarray(t[:, None] > t[None, :], BF16)

    def page_spec(gg):
        return pl.BlockSpec((None, None, H_F, page),
                            lambda b, s, pt: (layer, pt[b, (n_steps - 1 - s) * g + gg], 0, 0))

    grid_spec = pltpu.PrefetchScalarGridSpec(
        num_scalar_prefetch=1, grid=(bsz, n_steps),
        in_specs=[page_spec(gg) for gg in range(g)] +
                 [pl.BlockSpec((None, H_F, 1), lambda b, s, pt: (b, 0, 0)),
                  pl.BlockSpec((page, page), lambda b, s, pt: (0, 0))],
        out_specs=pl.BlockSpec((None, g, H_F, page), lambda b, s, pt: (b, n_steps - 1 - s, 0, 0)),
        scratch_shapes=[pltpu.VMEM((H_F, LANES), F32)])
    return pl.pallas_call(
        functools.partial(_fox_suffix_kernel, g=g), grid_spec=grid_spec,
        out_shape=jax.ShapeDtypeStruct((bsz, n_pages, H_F, page), F32),
        compiler_params=_cparams("arbitrary", "arbitrary"), name="fox_suffix",
    )(page_table, *([lf_t] * g), lf_new_cols, tri)


def _fox_decode_kernel(pt_ref, *refs, g):
    q_ref, k_refs, v_refs = refs[0], refs[1:1 + g], refs[1 + g:1 + 2 * g]
    bias_ref, kn_ref, vn_ref, o_ref, qb_ref, m_ref, l_ref, acc_ref = refs[1 + 2 * g:]
    step = pl.program_id(1)
    page = k_refs[0].shape[-1]

    @pl.when(step == 0)
    def _():
        qb_ref[...] = jnp.broadcast_to(q_ref[...], qb_ref.shape)
        m_ref[...] = jnp.full_like(m_ref, NEG_BIG)
        l_ref[...] = jnp.zeros_like(l_ref)
        acc_ref[...] = jnp.zeros_like(acc_ref)

    for h in range(H_F):
        qb = qb_ref[h]
        rows = [jnp.sum(k_refs[gg][h] * qb, axis=0, keepdims=True) + bias_ref[gg, h:h + 1, :] for gg in range(g)]
        s = jnp.concatenate(rows, axis=0)
        m_old = m_ref[h][:, 0:1]
        m_new = jnp.maximum(m_old, jnp.max(jnp.max(s, axis=1, keepdims=True), axis=0, keepdims=True))
        a = jnp.exp(m_old - m_new)
        p = jnp.exp(s - m_new)
        l_new = a * l_ref[h][:, 0:1] + jnp.sum(jnp.sum(p, axis=1, keepdims=True), axis=0, keepdims=True)
        acc = a * acc_ref[h]
        for gg in range(g):
            acc = acc + v_refs[gg][h] * p[gg:gg + 1, :]
        m_ref[h] = jnp.broadcast_to(m_new, (1, page))
        l_ref[h] = jnp.broadcast_to(l_new, (1, page))
        acc_ref[h] = acc

    @pl.when(step == pl.num_programs(1) - 1)
    def _():
        for h in range(H_F):
            m_run = m_ref[h][:, 0:1]
            s_new = jnp.sum(q_ref[h] * kn_ref[h], axis=0, keepdims=True)
            m_fin = jnp.maximum(m_run, s_new)
            a2 = jnp.exp(m_run - m_fin)
            p_new = jnp.exp(s_new - m_fin)
            num = a2 * jnp.sum(acc_ref[h], axis=1, keepdims=True) + p_new * vn_ref[h]
            o_ref[h] = num / (a2 * l_ref[h][:, 0:1] + p_new)


def _fox_decode(page_table, fqb, fk_new, fv_new, lf_new, cache_k, cache_v, cache_lf, layer):
    bsz, n_pages = page_table.shape
    page = cache_k.shape[2]
    g = _largest_divisor(n_pages, (8, 4, 2, 1))
    n_steps = n_pages // g
    k_t = jnp.transpose(cache_k, (0, 1, 3, 4, 2))
    v_t = jnp.transpose(cache_v, (0, 1, 3, 4, 2))
    lf_t = jnp.transpose(cache_lf, (0, 1, 3, 2))
    bias = _fox_suffix(page_table, lf_t, lf_new[:, :, None], layer, _largest_divisor(n_pages, (16, 8, 4, 2, 1)))
    col = lambda a: a.astype(F32).reshape(bsz, H_F, DH_F, 1)

    def page_spec(gg):
        return pl.BlockSpec((None, None, H_F, DH_F, page), lambda b, s, pt: (layer, pt[b, s * g + gg], 0, 0, 0))

    per_b = pl.BlockSpec((None, H_F, DH_F, 1), lambda b, s, pt: (b, 0, 0, 0))
    grid_spec = pltpu.PrefetchScalarGridSpec(
        num_scalar_prefetch=1, grid=(bsz, n_steps),
        in_specs=[per_b] + [page_spec(gg) for gg in range(g)] * 2 +
                 [pl.BlockSpec((None, g, H_F, page), lambda b, s, pt: (b, s, 0, 0)), per_b, per_b],
        out_specs=per_b,
        scratch_shapes=[pltpu.VMEM((H_F, DH_F, page), F32), pltpu.VMEM((H_F, 1, page), F32),
                        pltpu.VMEM((H_F, 1, page), F32), pltpu.VMEM((H_F, DH_F, page), F32)])
    out = pl.pallas_call(
        functools.partial(_fox_decode_kernel, g=g), grid_spec=grid_spec,
        out_shape=jax.ShapeDtypeStruct((bsz, H_F, DH_F, 1), F32),
        compiler_params=_cparams("arbitrary", "arbitrary"), name="fox_decode",
    )(page_table, col(fqb), *([k_t] * g), *([v_t] * g), bias, col(fk_new), col(fv_new))
    return out.reshape(bsz, H_F * DH_F)


def _mlstm_prompt_kernel(q_ref, k_ref, v_ref, cols_ref, ccols_ref, rows_ref, h_ref, cst_ref, mst_ref,
                         c_scr, m_scr, *, n_chunks, ln):
    i = pl.program_id(0)

    @pl.when(i == 0)
    def _():
        c_scr[...] = jnp.zeros_like(c_scr)
        m_scr[...] = jnp.zeros_like(m_scr)

    lane = lax.broadcasted_iota(I32, (ln, LANES), 1)
    one_col = jnp.where(lane == 0, 1.0, 0.0).astype(F32)
    tril = lax.broadcasted_iota(I32, (ln, ln), 1) <= lax.broadcasted_iota(I32, (ln, ln), 0)
    srow = lax.broadcasted_iota(I32, (LANES, 2 * LANES), 0)

    def chunk(c, _):
        t0 = pl.multiple_of(c * ln, ln)
        cols = cols_ref[pl.ds(t0, ln), :]
        ccols = ccols_ref[pl.ds(t0, ln), :]
        rows = rows_ref[c]
        for p in range(H_M // 2):
            q2 = q_ref[pl.ds(t0, ln), p * LANES:(p + 1) * LANES]
            k2 = k_ref[pl.ds(t0, ln), p * LANES:(p + 1) * LANES].astype(BF16)
            c_pair = c_scr[p]
            c_pair_b = c_pair.astype(BF16)
            new_pair = c_pair
            for hh in range(2):
                h = 2 * p + hh
                own = (lane < DK_M) if hh == 0 else (lane >= DK_M)
                qm = jnp.where(own, q2, 0.0).astype(BF16)
                v = v_ref[pl.ds(t0, ln), h * DV_M:(h + 1) * DV_M]
                v_aug = jnp.concatenate([v, one_col], axis=1)
                ig_col = cols[:, 8 + h:9 + h]
                bc_col = ccols[:, 12 + h:13 + h]
                ig_row = rows[8 + h:9 + h, :]
                bc_row = rows[28 + h:29 + h, :]
                m0 = m_scr[h][0:1, 0:1]
                d = jnp.where(tril, bc_col - bc_row + ig_row, -jnp.inf)
                inter = bc_col + m0
                m_row = jnp.maximum(jnp.max(d, axis=1, keepdims=True), inter)
                s_in = jnp.exp(inter - m_row)
                sq = _dot_nt(qm, k2) * jnp.exp(d - m_row)
                num_aug = _dot(sq.astype(BF16), v_aug.astype(BF16)) + s_in * _dot(qm, c_pair_b)
                den = num_aug[:, DV_M:DV_M + 1]
                hv = num_aug[:, :DV_M] / jnp.maximum(jnp.abs(den), jnp.exp(-m_row))
                h_ref[pl.ds(t0, ln), h * DV_M:(h + 1) * DV_M] = hv
                b_last = bc_col[ln - 1:ln, :]
                dec = b_last - bc_col + ig_col
                m_new = jnp.maximum(b_last + m0, jnp.max(dec, axis=0, keepdims=True))
                ws = jnp.exp(dec - m_new)
                s0 = jnp.exp(b_last + m0 - m_new)
                upd = s0 * c_pair + _dot_tn(k2, (ws * v_aug).astype(BF16))
                rows_own = (srow < DK_M) if hh == 0 else (srow >= DK_M)
                new_pair = jnp.where(rows_own, upd, new_pair)
                m_scr[h] = jnp.broadcast_to(m_new, (8, LANES))
            c_scr[p] = new_pair
        return 0

    lax.fori_loop(0, n_chunks, chunk, 0)
    cst_ref[...] = c_scr[...]
    mst_ref[...] = m_scr[...]


def _mlstm_prompt(mq, mk, mv, cols, ccols, rows, g):
    s = mq.shape[0]
    ln = M_CHUNK
    tb = g * ln
    rows_ch = rows.reshape(32, s // ln, ln).transpose(1, 0, 2)
    rowb = lambda n: pl.BlockSpec((tb, n), lambda i: (i, 0))
    h, cst, mst = pl.pallas_call(
        functools.partial(_mlstm_prompt_kernel, n_chunks=g, ln=ln), grid=(s // tb,),
        in_specs=[rowb(256), rowb(256), rowb(512), rowb(LANES), rowb(LANES),
                  pl.BlockSpec((g, 32, ln), lambda i: (i, 0, 0))],
        out_specs=[rowb(512), pl.BlockSpec((2, LANES, 2 * LANES), lambda i: (0, 0, 0)),
                   pl.BlockSpec((H_M, 8, LANES), lambda i: (0, 0, 0))],
        out_shape=[jax.ShapeDtypeStruct((s, 512), F32), jax.ShapeDtypeStruct((2, LANES, 2 * LANES), F32),
                   jax.ShapeDtypeStruct((H_M, 8, LANES), F32)],
        scratch_shapes=[pltpu.VMEM((2, LANES, 2 * LANES), F32), pltpu.VMEM((H_M, 8, LANES), F32)],
        compiler_params=_cparams("arbitrary"), name="mlstm_prompt",
    )(mq, mk, mv, cols, ccols, rows_ch)
    c_new = cst[:, :, :DV_M].reshape(H_M, DK_M, DV_M)
    n_new = cst[:, :, DV_M].reshape(H_M, DK_M)
    m_new = mst[:, 0, 0]
    return h, c_new, n_new, m_new


def _mlstm_step_kernel(q_ref, k_ref, v_ref, cols_ref, c_ref, n_ref, m_ref, h_ref, cn_ref, nn_ref, mn_ref):
    cols = cols_ref[...]
    lane = lax.broadcasted_iota(I32, (1, LANES), 1)
    lane8 = lax.broadcasted_iota(I32, (8, LANES), 1)
    row8 = lax.broadcasted_iota(I32, (8, LANES), 0)
    srow = lax.broadcasted_iota(I32, (LANES, 1), 0)
    m_out = jnp.zeros((1, LANES), F32)
    for p in range(H_M // 2):
        q2 = q_ref[:, p * LANES:(p + 1) * LANES]
        k2 = k_ref[:, p * LANES:(p + 1) * LANES]
        q2r = q2.astype(BF16).astype(F32)
        k2r = k2.astype(BF16).astype(F32)
        n_pair = n_ref[:, p * LANES:(p + 1) * LANES]
        c_pair = c_ref[p * LANES:(p + 1) * LANES, :]
        c_pair_b = c_pair.astype(BF16)
        upd = jnp.zeros((LANES, LANES), F32)
        s0_rows = jnp.zeros((LANES, 1), F32)
        n_new = jnp.zeros((1, LANES), F32)
        for hh in range(2):
            h = 2 * p + hh
            own = (lane < DK_M) if hh == 0 else (lane >= DK_M)
            own8 = (lane8 < DK_M) if hh == 0 else (lane8 >= DK_M)
            qf = jnp.where(own, q2r, 0.0)
            v = v_ref[:, h * DV_M:(h + 1) * DV_M]
            ig = cols[:, 8 + h:9 + h]
            lf = cols[:, 12 + h:13 + h]
            m0 = m_ref[:, h:h + 1]
            inter = lf + m0
            m_row = jnp.maximum(ig, inter)
            s_in = jnp.exp(inter - m_row)
            wqk = jnp.sum(qf * k2r, axis=1, keepdims=True) * jnp.exp(ig - m_row)
            q_c = _dot(jnp.broadcast_to(qf, (8, LANES)).astype(BF16), c_pair_b)[0:1, :]
            q_n = jnp.sum(qf * n_pair.astype(BF16).astype(F32), axis=1, keepdims=True)
            num = wqk * v.astype(BF16).astype(F32) + s_in * q_c
            den = wqk + s_in * q_n
            h_ref[:, h * DV_M:(h + 1) * DV_M] = num / jnp.maximum(jnp.abs(den), jnp.exp(-m_row))
            ws = jnp.exp(ig - m_row)
            km8 = jnp.where(row8 == 0, jnp.where(own8, jnp.broadcast_to(k2r, (8, LANES)), 0.0), 0.0).astype(BF16)
            wv8 = jnp.where(row8 == 0, jnp.broadcast_to(ws * v, (8, LANES)), 0.0).astype(BF16)
            upd = upd + _dot_tn(km8, wv8)
            s0_rows = jnp.where((srow < DK_M) if hh == 0 else (srow >= DK_M), s_in, s0_rows)
            n_new = jnp.where(own, s_in * n_pair + ws * k2, n_new)
            m_out = jnp.where(lane == h, m_row, m_out)
        cn_ref[p * LANES:(p + 1) * LANES, :] = s0_rows * c_pair + upd
        nn_ref[:, p * LANES:(p + 1) * LANES] = n_new
    mn_ref[...] = m_out


def _mlstm_step(mq, mk, mv, cols, c0, n0, m0):
    bsz = mq.shape[0]
    b3 = lambda r, n: pl.BlockSpec((None, r, n), lambda i: (i, 0, 0))
    h, cn, nn, mn = pl.pallas_call(
        _mlstm_step_kernel, grid=(bsz,),
        in_specs=[b3(1, 256), b3(1, 256), b3(1, 512), b3(1, LANES), b3(H_M * DK_M, DV_M), b3(1, 256), b3(1, H_M)],
        out_specs=[b3(1, 512), b3(H_M * DK_M, DV_M), b3(1, 256), b3(1, LANES)],
        out_shape=[jax.ShapeDtypeStruct((bsz, 1, 512), F32), jax.ShapeDtypeStruct((bsz, H_M * DK_M, DV_M), F32),
                   jax.ShapeDtypeStruct((bsz, 1, 256), F32), jax.ShapeDtypeStruct((bsz, 1, LANES), F32)],
        compiler_params=_cparams("arbitrary"), name="mlstm_step",
    )(mq[:, None, :], mk[:, None, :], mv[:, None, :], cols[:, None, :],
      c0.reshape(bsz, H_M * DK_M, DV_M), n0.reshape(bsz, 1, H_M * DK_M), m0.reshape(bsz, 1, H_M))
    return (h[:, 0, :], cn.reshape(bsz, H_M, DK_M, DV_M), nn.reshape(bsz, H_M, DK_M), mn[:, 0, :H_M])


def _merge_kernel(x_ref, cy_ref, fy_ref, mh_ref, mo_ref, wg_ref, bg_ref, gm_ref, wb_ref, wo_ref, g_ref, b_ref, o_ref):
    x = x_ref[...]
    gates = jax.nn.sigmoid(_dot(x.astype(BF16), wg_ref[...]) + bg_ref[...])
    segs = []
    for h in range(H_M):
        seg = mh_ref[:, h * DV_M:(h + 1) * DV_M]
        segs.append(seg * lax.rsqrt(jnp.mean(seg * seg, axis=-1, keepdims=True) + LN_EPS))
    my = (jnp.concatenate(segs, axis=1) * gm_ref[...] * mo_ref[...]).astype(BF16)
    mix = gates[:, :D_MODEL] * _dot(cy_ref[...], wb_ref[0])
    mix = mix + gates[:, D_MODEL:2 * D_MODEL] * _dot(fy_ref[...], wb_ref[1])
    mix = mix + gates[:, 2 * D_MODEL:] * _dot(my, wb_ref[2])
    y = _dot(mix.astype(BF16), wo_ref[...])
    o_ref[...] = _layer_norm(ALPHA * x + y, g_ref[...], b_ref[...])


def _merge(x, conv_y, fox_y, m_h, m_o, w, tm):
    r = x.shape[0]
    rowb = lambda n: pl.BlockSpec((tm, n), lambda i: (i, 0))
    full = lambda a: pl.BlockSpec(a.shape, lambda i: (0,) * a.ndim)
    consts = [w['wg'], w['bg'], w['g_mnorm'], w['w_branch'], w['w_mix_out'], w['g_ln1'], w['b_ln1']]
    return pl.pallas_call(
        _merge_kernel, grid=(r // tm,),
        in_specs=[rowb(D_MODEL), rowb(512), rowb(512), rowb(512), rowb(512)] + [full(a) for a in consts],
        out_specs=rowb(D_MODEL), out_shape=jax.ShapeDtypeStruct((r, D_MODEL), F32),
        compiler_params=_cparams("arbitrary"), name="merge",
    )(x, conv_y, fox_y, m_h, m_o, *consts)


def _memkv_kernel(mem_ref, wk_ref, wv_ref, k_ref, v_ref, kb_ref, vb_ref):
    mb = mem_ref[...].astype(BF16)
    k = _dot(mb, wk_ref[...])
    v = _dot(mb, wv_ref[...])
    k_ref[...] = k
    v_ref[...] = v
    kb_ref[...] = k.astype(BF16)
    vb_ref[...] = v.astype(BF16)


def _memkv(mem, wk, wv):
    n = mem.shape[0]
    full = lambda a: pl.BlockSpec(a.shape, lambda i: (0,) * a.ndim)
    sd = lambda dt: jax.ShapeDtypeStruct((n, D_MODEL), dt)
    return pl.pallas_call(
        _memkv_kernel, grid=(1,),
        in_specs=[full(mem), full(wk), full(wv)],
        out_specs=[pl.BlockSpec((n, D_MODEL), lambda i: (0, 0))] * 4,
        out_shape=[sd(F32), sd(F32), sd(BF16), sd(BF16)],
        compiler_params=_cparams("arbitrary"), name="memkv",
    )(mem, wk, wv)


def _xattn_prompt_kernel(x_ref, kb_ref, vb_ref, wq_ref, wo_ref, g_ref, b_ref, o_ref):
    x = x_ref[...]
    q = (_dot(x.astype(BF16), wq_ref[...]) * DH_X ** -0.5).astype(BF16)
    outs = []
    for h in range(H_X):
        sl = slice(h * DH_X, (h + 1) * DH_X)
        s = _dot_nt(q[:, sl], kb_ref[:, sl])
        s = s - jnp.max(s, axis=1, keepdims=True)
        p = jnp.exp(s)
        o = _dot(p.astype(BF16), vb_ref[:, sl]) / jnp.sum(p, axis=1, keepdims=True)
        outs.append(o.astype(BF16))
    y = _dot(jnp.concatenate(outs, axis=1), wo_ref[...])
    o_ref[...] = _layer_norm(ALPHA * x + y, g_ref[...], b_ref[...])


def _xattn_prompt(x, kb, vb, w, tm):
    r = x.shape[0]
    rowb = pl.BlockSpec((tm, D_MODEL), lambda i: (i, 0))
    full = lambda a: pl.BlockSpec(a.shape, lambda i: (0,) * a.ndim)
    consts = [kb, vb, w['w_xq'], w['w_xo'], w['g_ln2'], w['b_ln2']]
    return pl.pallas_call(
        _xattn_prompt_kernel, grid=(r // tm,),
        in_specs=[rowb] + [full(a) for a in consts],
        out_specs=rowb, out_shape=jax.ShapeDtypeStruct((r, D_MODEL), F32),
        compiler_params=_cparams("arbitrary"), name="xattn_prompt",
    )(x, *consts)


def _linear_kernel(x_ref, w_ref, o_ref, *, scale):
    o_ref[...] = (_dot(x_ref[...].astype(BF16), w_ref[...]) * scale).astype(o_ref.dtype)


def _linear(x, w, scale, out_dtype):
    r, n = x.shape[0], w.shape[1]
    full = lambda a: pl.BlockSpec(a.shape, lambda i: (0,) * a.ndim)
    return pl.pallas_call(
        functools.partial(_linear_kernel, scale=scale), grid=(1,),
        in_specs=[full(x), full(w)], out_specs=pl.BlockSpec((r, n), lambda i: (0, 0)),
        out_shape=jax.ShapeDtypeStruct((r, n), out_dtype),
        compiler_params=_cparams("arbitrary"), name="linear",
    )(x, w)


def _linear_ln_kernel(x_ref, a_ref, w_ref, g_ref, b_ref, o_ref):
    y = _dot(a_ref[...].astype(BF16), w_ref[...])
    o_ref[...] = _layer_norm(ALPHA * x_ref[...] + y, g_ref[...], b_ref[...])


def _linear_ln(x, a, w, g, b):
    full = lambda t: pl.BlockSpec(t.shape, lambda i: (0,) * t.ndim)
    return pl.pallas_call(
        _linear_ln_kernel, grid=(1,),
        in_specs=[full(x), full(a), full(w), full(g), full(b)],
        out_specs=pl.BlockSpec(x.shape, lambda i: (0, 0)),
        out_shape=jax.ShapeDtypeStruct(x.shape, F32),
        compiler_params=_cparams("arbitrary"), name="linear_ln",
    )(x, a, w, g, b)


def _xattn_decode_kernel(q_ref, k_ref, v_ref, o_ref):
    w = q_ref.shape[-1]
    row = lax.broadcasted_iota(I32, (8, w), 0)
    lane = lax.broadcasted_iota(I32, (8, w), 1)
    own = (lane // DH_X) == row
    qrows = jnp.where(own, jnp.broadcast_to(q_ref[...].astype(F32), (8, w)), 0.0).astype(BF16)
    s = _dot_nt(qrows, k_ref[...].astype(BF16))
    s = s - jnp.max(s, axis=1, keepdims=True)
    p = jnp.exp(s)
    o = _dot(p.astype(BF16), v_ref[...].astype(BF16)) / jnp.sum(p, axis=1, keepdims=True)
    o_ref[...] = jnp.sum(jnp.where(own, o, 0.0), axis=0, keepdims=True)


def _xattn_decode(q, mem_k, mem_v):
    bsz, n_mem = mem_k.shape[0], mem_k.shape[1]
    b3 = lambda r: pl.BlockSpec((None, r, D_MODEL), lambda i: (i, 0, 0))
    out = pl.pallas_call(
        _xattn_decode_kernel, grid=(bsz,),
        in_specs=[b3(1), b3(n_mem), b3(n_mem)], out_specs=b3(1),
        out_shape=jax.ShapeDtypeStruct((bsz, 1, D_MODEL), F32),
        compiler_params=_cparams("arbitrary"), name="xattn_decode",
    )(q[:, None, :], mem_k.reshape(bsz, n_mem, D_MODEL), mem_v.reshape(bsz, n_mem, D_MODEL))
    return out[:, 0, :]


def _router_kernel(x_ref, w_ref, b_ref, lst_ref, ti_ref, tg_ref, cnt_ref, carry_ref):
    i = pl.program_id(0)

    @pl.when(i == 0)
    def _():
        carry_ref[...] = jnp.zeros_like(carry_ref)

    x1, x2, x3 = _split3(x_ref[...])
    w1, w2, w3 = w_ref[0], w_ref[1], w_ref[2]
    logits = (_dot(x1, w1) + (_dot(x1, w2) + _dot(x2, w1)) + (_dot(x1, w3) + _dot(x2, w2) + _dot(x3, w1))) + b_ref[...]
    tm = logits.shape[0]
    lane = lax.broadcasted_iota(I32, (tm, LANES), 1)
    lane_f = lane.astype(F32)
    vals = logits
    tops, idxs, sels = [], [], []
    for _ in range(TOP_K):
        mx = jnp.max(vals, axis=1, keepdims=True)
        idx = jnp.min(jnp.where(vals == mx, lane_f, float(LANES)), axis=1, keepdims=True)
        sel = lane_f == idx
        tops.append(mx)
        idxs.append(idx)
        sels.append(sel)
        vals = jnp.where(sel, -jnp.inf, vals)
    exps = [jnp.exp(t - tops[0]) for t in tops]
    den = exps[0] + exps[1] + exps[2] + exps[3]
    cnt = jnp.zeros((tm, LANES), F32)
    for sel in sels:
        cnt = cnt + jnp.where(sel, 1.0, 0.0)
    excl = _dot(lst_ref[...], cnt.astype(BF16)) + carry_ref[0:1, :]
    ti = jnp.zeros((tm, LANES), F32)
    tg = jnp.zeros((tm, LANES), F32)
    for k in range(TOP_K):
        rank = jnp.sum(jnp.where(sels[k], excl, 0.0), axis=1, keepdims=True)
        ti = jnp.where(lane == k, idxs[k], ti)
        ti = jnp.where(lane == TOP_K + k, rank, ti)
        tg = jnp.where(lane == k, exps[k] / den, tg)
    ti_ref[...] = ti.astype(I32)
    tg_ref[...] = tg
    carry_ref[...] = carry_ref[...] + jnp.sum(cnt, axis=0, keepdims=True)
    cnt_ref[...] = carry_ref[...]


def _router(x, w3, b, tm):
    r = x.shape[0]
    t = np.arange(tm)
    lst = jnp.asarray(t[:, None] > t[None, :], BF16)
    full = lambda a: pl.BlockSpec(a.shape, lambda i: (0,) * a.ndim)
    rowb = lambda n: pl.BlockSpec((tm, n), lambda i: (i, 0))
    return pl.pallas_call(
        _router_kernel, grid=(r // tm,),
        in_specs=[rowb(D_MODEL), full(w3), full(b), full(lst)],
        out_specs=[rowb(LANES), rowb(LANES), pl.BlockSpec((8, LANES), lambda i: (0, 0))],
        out_shape=[jax.ShapeDtypeStruct((r, LANES), I32), jax.ShapeDtypeStruct((r, LANES), F32),
                   jax.ShapeDtypeStruct((8, LANES), F32)],
        scratch_shapes=[pltpu.VMEM((8, LANES), F32)],
        compiler_params=_cparams("arbitrary"), name="router",
    )(x, w3, b, lst)


def _moe_dispatch_kernel(idx_hbm, x_ref, xs_in, xs_hbm, idx_smem, sem_idx, sem):
    del xs_in
    i = pl.program_id(0)
    tm = x_ref.shape[0]
    idx_cp = pltpu.make_async_copy(idx_hbm.at[i], idx_smem, sem_idx)
    idx_cp.start()
    idx_cp.wait()

    def send(r, _):
        for k in range(TOP_K):
            pltpu.make_async_copy(x_ref.at[pl.ds(r, 1)], xs_hbm.at[pl.ds(idx_smem[r * TOP_K + k], 1)], sem).start()
        return 0

    lax.fori_loop(0, tm, send, 0)
    for k in range(TOP_K):
        pltpu.make_async_copy(x_ref, xs_hbm.at[pl.ds(0, tm)], sem).wait()


def _moe_dispatch(idx, x, rows, tm):
    n = x.shape[0]
    return pl.pallas_call(
        _moe_dispatch_kernel, grid=(n // tm,),
        in_specs=[pl.BlockSpec(memory_space=pl.ANY), pl.BlockSpec((tm, D_MODEL), lambda i: (i, 0)),
                  pl.BlockSpec(memory_space=pl.ANY)],
        out_specs=pl.BlockSpec(memory_space=pl.ANY),
        out_shape=jax.ShapeDtypeStruct((rows, D_MODEL), F32),
        scratch_shapes=[pltpu.SMEM((tm * TOP_K,), I32), pltpu.SemaphoreType.DMA(()), pltpu.SemaphoreType.DMA(())],
        input_output_aliases={2: 0},
        compiler_params=_cparams("arbitrary"), name="moe_dispatch",
    )(idx, x, jnp.zeros((rows, D_MODEL), F32))


def _moe_ffn_kernel(be_ref, nv_ref, xs_ref, wgu_ref, bgu_ref, wd_ref, bd_ref, y_ref, wgu_b, wd_b):
    i = pl.program_id(0)

    @pl.when(i < nv_ref[0])
    def _():
        e = be_ref[i]
        e_prev = be_ref[jnp.maximum(i - 1, 0)]

        @pl.when((i == 0) | (e != e_prev))
        def _():
            wgu_b[...] = wgu_ref[...].astype(BF16)
            wd_b[...] = wd_ref[...].astype(BF16)

        gu = _dot(xs_ref[...].astype(BF16), wgu_b[...]) + bgu_ref[...]
        g = jnp.minimum(gu[:, :D_EXPERT], SWIGLU_LIMIT)
        u = jnp.clip(gu[:, D_EXPERT:], -SWIGLU_LIMIT, SWIGLU_LIMIT)
        act = (u + 1.0) * g * jax.nn.sigmoid(SWIGLU_ALPHA * g)
        y_ref[...] = _dot(act.astype(BF16), wd_b[...]) + bd_ref[...]

    @pl.when(i >= nv_ref[0])
    def _():
        y_ref[...] = jnp.zeros_like(y_ref)


def _moe_ffn(blk_e, n_valid, xs, w, l, tb):
    rows = xs.shape[0]
    wsel = lambda i, be, nv: (l, be[i], 0, 0)
    grid_spec = pltpu.PrefetchScalarGridSpec(
        num_scalar_prefetch=2, grid=(rows // tb,),
        in_specs=[pl.BlockSpec((tb, D_MODEL), lambda i, be, nv: (jnp.minimum(i, nv[0] - 1), 0)),
                  pl.BlockSpec((None, None, D_MODEL, 2 * D_EXPERT), wsel),
                  pl.BlockSpec((None, None, 1, 2 * D_EXPERT), wsel),
                  pl.BlockSpec((None, None, D_EXPERT, D_MODEL), wsel),
                  pl.BlockSpec((None, None, 1, D_MODEL), wsel)],
        out_specs=pl.BlockSpec((tb, D_MODEL), lambda i, be, nv: (i, 0)),
        scratch_shapes=[pltpu.VMEM((D_MODEL, 2 * D_EXPERT), BF16), pltpu.VMEM((D_EXPERT, D_MODEL), BF16)])
    return pl.pallas_call(
        _moe_ffn_kernel, grid_spec=grid_spec,
        out_shape=jax.ShapeDtypeStruct((rows, D_MODEL), F32),
        compiler_params=_cparams("arbitrary"), name="moe_ffn",
    )(blk_e, n_valid, xs, w['w_gate_up'], w['b_gate_up'][:, :, None, :], w['w_down'], w['b_down'][:, :, None, :])


def _moe_combine_kernel(idx_hbm, x_ref, tg_ref, ys_hbm, g_ref, b_ref, o_ref, idx_smem, ybuf, sem_idx, sem):
    i = pl.program_id(0)
    tm = x_ref.shape[0]
    idx_cp = pltpu.make_async_copy(idx_hbm.at[i], idx_smem, sem_idx)
    idx_cp.start()
    idx_cp.wait()

    def fetch(r, _):
        for k in range(TOP_K):
            pltpu.make_async_copy(ys_hbm.at[pl.ds(idx_smem[r * TOP_K + k], 1)], ybuf.at[k, pl.ds(r, 1)], sem).start()
        return 0

    lax.fori_loop(0, tm, fetch, 0)
    for k in range(TOP_K):
        pltpu.make_async_copy(ys_hbm.at[pl.ds(0, tm)], ybuf.at[k], sem).wait()
    tg = tg_ref[...]
    ff = tg[:, 0:1] * ybuf[0]
    for k in range(1, TOP_K):
        ff = ff + tg[:, k:k + 1] * ybuf[k]
    o_ref[...] = _layer_norm(ALPHA * x_ref[...] + ff, g_ref[...], b_ref[...])


def _moe_combine(idx, x, tg, ys, g, b, tm):
    n = x.shape[0]
    rowb = lambda wd: pl.BlockSpec((tm, wd), lambda i: (i, 0))
    full = lambda t: pl.BlockSpec(t.shape, lambda i: (0,) * t.ndim)
    return pl.pallas_call(
        _moe_combine_kernel, grid=(n // tm,),
        in_specs=[pl.BlockSpec(memory_space=pl.ANY), rowb(D_MODEL), rowb(LANES), pl.BlockSpec(memory_space=pl.ANY),
                  full(g), full(b)],
        out_specs=rowb(D_MODEL), out_shape=jax.ShapeDtypeStruct((n, D_MODEL), F32),
        scratch_shapes=[pltpu.SMEM((tm * TOP_K,), I32), pltpu.VMEM((TOP_K, tm, D_MODEL), F32),
                        pltpu.SemaphoreType.DMA(()), pltpu.SemaphoreType.DMA(())],
        compiler_params=_cparams("arbitrary"), name="moe_combine",
    )(idx, x, tg, ys, g, b)


def _moe(x, w, l, tb, tm):
    n = x.shape[0]
    a = n * TOP_K
    ti, tg, cnt = _router(x, w['w_router3'], w['b_router'], tm)
    top_e = ti[:, :TOP_K]
    rank = ti[:, TOP_K:2 * TOP_K]
    counts = cnt[0, :N_EXPERTS].astype(I32)
    n_blk = -(-a // tb) + N_EXPERTS
    padded = (counts + tb - 1) // tb * tb
    pend = jnp.cumsum(padded)
    dest = (pend - padded)[top_e] + rank
    idx = dest.reshape(n // tm, tm * TOP_K)
    n_valid = (pend[-1:] // tb).astype(I32)
    blk_id = jnp.minimum(jnp.arange(n_blk, dtype=I32), n_valid - 1)
    blk_e = jnp.minimum(jnp.sum((pend[None, :] <= (blk_id * tb)[:, None]).astype(I32), axis=1), N_EXPERTS - 1)
    xs = _moe_dispatch(idx, x, n_blk * tb, tm)
    ys = _moe_ffn(blk_e, n_valid, xs, w, l, tb)
    return _moe_combine(idx, x, tg, ys, w['g_ln3'], w['b_ln3'], tm)


def _layer_weights(l, w_in, b_in, b_fox_f, b_mlstm_f, g_mlstm_norm, w_branch, w_mix_out, g_ln1, b_ln1,
                   w_xq, w_xk, w_xv, w_xo, g_ln2, b_ln2, w_router, b_router, w_gate_up, b_gate_up,
                   w_down, b_down, g_ln3, b_ln3):
    w = _split_w_in(w_in[l], b_in[l], b_fox_f[l], b_mlstm_f[l])
    wr = jnp.pad(w_router[l], ((0, 0), (0, LANES - N_EXPERTS)))
    r1 = wr.astype(BF16)
    r2 = (wr - r1.astype(F32)).astype(BF16)
    r3 = (wr - r1.astype(F32) - r2.astype(F32)).astype(BF16)
    w.update(
        g_mnorm=g_mlstm_norm[l][None], w_branch=w_branch[l].astype(BF16), w_mix_out=w_mix_out[l].astype(BF16),
        g_ln1=g_ln1[l][None], b_ln1=b_ln1[l][None],
        w_xq=w_xq[l].astype(BF16), w_xk=w_xk[l].astype(BF16), w_xv=w_xv[l].astype(BF16), w_xo=w_xo[l].astype(BF16),
        g_ln2=g_ln2[l][None], b_ln2=b_ln2[l][None],
        w_router3=jnp.stack([r1, r2, r3]),
        b_router=jnp.pad(b_router[l], (0, LANES - N_EXPERTS), constant_values=NEG_BIG)[None],
        w_gate_up=w_gate_up, b_gate_up=b_gate_up, w_down=w_down, b_down=b_down,
        g_ln3=g_ln3[l][None], b_ln3=b_ln3[l][None])
    return w


def kernel(x_prompt, x_sample, mem_prompt, cache_fox_k, cache_fox_v, cache_fox_lf, page_table, state_conv,
           state_mlstm_c, state_mlstm_n, state_mlstm_m, cache_mem_k, cache_mem_v, w_in, b_in, b_fox_f, b_mlstm_f,
           w_dw, b_dw, g_conv_ln, b_conv_ln, g_mlstm_norm, w_branch, w_mix_out, g_ln1, b_ln1, w_xq, w_xk, w_xv,
           w_xo, g_ln2, b_ln2, w_router, b_router, w_gate_up, b_gate_up, w_down, b_down, g_ln3, b_ln3):
    b_p, s, _ = x_prompt.shape
    b_s = x_sample.shape[0]
    assert b_p == 1 and x_sample.shape[1] == 1
    depth = w_in.shape[0]
    xp = x_prompt.reshape(s, D_MODEL)
    xs = x_sample.reshape(b_s, D_MODEL)
    mem = mem_prompt.reshape(-1, D_MODEL)
    tm = _row_block(s, 256)
    tq = _row_block(s, 256)
    tk = _row_block(s, 1024)
    g_chunks = 8 if s % (8 * M_CHUNK) == 0 else 1
    tb_p = 256 if s * TOP_K >= 256 * N_EXPERTS else 8
    rows_p, rows_s = [], []
    for l in range(depth):
        w = _layer_weights(l, w_in, b_in, b_fox_f, b_mlstm_f, g_mlstm_norm, w_branch, w_mix_out, g_ln1, b_ln1,
                           w_xq, w_xk, w_xv, w_xo, g_ln2, b_ln2, w_router, b_router, w_gate_up, b_gate_up,
                           w_down, b_down, g_ln3, b_ln3)
        pr = _inproj(xp, w, tm)
        conv_y = _conv_prompt(pr['u'], w_dw[l], b_dw[l], g_conv_ln[l], b_conv_ln[l], tm)
        fox_y = _fox_prompt(pr['fqb'], pr['fkb'], pr['fvb'], pr['rows'][16:24], tq, tk)
        m_h, m_c, m_n, m_m = _mlstm_prompt(pr['mq'], pr['mk'], pr['mv'], pr['cols'], pr['ccols'], pr['rows'], g_chunks)
        xp = _merge(xp, conv_y, fox_y, m_h, pr['mo'], w, tm)
        mem_k, mem_v, mem_kb, mem_vb = _memkv(mem, w['w_xk'], w['w_xv'])
        xp = _xattn_prompt(xp, mem_kb, mem_vb, w, tm)
        xp = _moe(xp, w, l, tb_p, tm)
        rows_p.append((pr['fk'].reshape(1, s, H_F, DH_F), pr['fv'].reshape(1, s, H_F, DH_F),
                       pr['cols'][:, :H_F].reshape(1, s, H_F), pr['u'][s - (K_CONV - 1):][None],
                       m_c[None], m_n[None], m_m[None],
                       mem_k.reshape(1, -1, H_X, DH_X), mem_v.reshape(1, -1, H_X, DH_X)))
        sr = _inproj(xs, w, b_s)
        conv_y, conv_st = _conv_step(state_conv[l], sr['u'], w_dw[l], b_dw[l], g_conv_ln[l], b_conv_ln[l])
        lf_new = sr['cols'][:, :H_F]
        fox_y = _fox_decode(page_table, sr['fqb'], sr['fk'], sr['fv'], lf_new, cache_fox_k, cache_fox_v,
                            cache_fox_lf, l)
        m_h, m_c, m_n, m_m = _mlstm_step(sr['mq'], sr['mk'], sr['mv'], sr['cols'], state_mlstm_c[l],
                                         state_mlstm_n[l], state_mlstm_m[l])
        xs = _merge(xs, conv_y, fox_y.astype(BF16), m_h, sr['mo'], w, b_s)
        q = _linear(xs, w['w_xq'], DH_X ** -0.5, BF16)
        att = _xattn_decode(q, cache_mem_k[l], cache_mem_v[l])
        xs = _linear_ln(xs, att, w['w_xo'], w['g_ln2'], w['b_ln2'])
        xs = _moe(xs, w, l, 8, b_s)
        rows_s.append((sr['fk'].reshape(b_s, 1, H_F, DH_F), sr['fv'].reshape(b_s, 1, H_F, DH_F),
                       lf_new.reshape(b_s, 1, H_F), conv_st, m_c, m_n, m_m))
    outs_p = [jnp.stack(a) for a in zip(*rows_p)]
    outs_s = [jnp.stack(a) for a in zip(*rows_s)]
    return (xp.reshape(1, s, D_MODEL), xs.reshape(b_s, 1, D_MODEL), *outs_p, *outs_s)
```

```python
import functools

import numpy as np
import jax
import jax.numpy as jnp
from jax import lax
from jax.experimental import pallas as pl
from jax.experimental.pallas import tpu as pltpu

F32 = jnp.float32
BF16 = jnp.bfloat16
I32 = jnp.int32

D_MODEL = 1024
W_BR = 512
K_CONV = 31
H_F = 8
DH_F = 64
H_M = 4
DK_M = 64
DV_M = 128
M_CHUNK = 64
I_CAP = 15.0
H_X = 4
DH_X = 256
N_EXPERTS = 32
TOP_K = 4
D_EXPERT = 1024
SWIGLU_ALPHA = 1.702
SWIGLU_LIMIT = 7.0
LN_EPS = 1e-5
DEPTH = 2
ALPHA = (2.0 * DEPTH) ** 0.25
NEG_BIG = -1e30
LANES = 128
VMEM_LIMIT = 56 * 1024 * 1024


def _cparams(*sem):
    return pltpu.CompilerParams(dimension_semantics=sem, vmem_limit_bytes=VMEM_LIMIT)


def _dot(a, b):
    return jnp.dot(a, b, preferred_element_type=F32)


def _dot_nt(a, b):
    return lax.dot_general(a, b, (((1,), (1,)), ((), ())), preferred_element_type=F32)


def _dot_tn(a, b):
    return lax.dot_general(a, b, (((0,), (0,)), ((), ())), preferred_element_type=F32)


def _split3(a):
    a1 = a.astype(BF16)
    r = a - a1.astype(F32)
    a2 = r.astype(BF16)
    r = r - a2.astype(F32)
    return a1, a2, r.astype(BF16)


def _log_sigmoid(x):
    return jnp.minimum(x, 0.0) - jnp.log1p(jnp.exp(-jnp.abs(x)))


def _layer_norm(x, g, b):
    mu = jnp.mean(x, axis=-1, keepdims=True)
    xc = x - mu
    var = jnp.mean(xc * xc, axis=-1, keepdims=True)
    return xc * lax.rsqrt(var + LN_EPS) * g + b


def _row_block(n, pref):
    return pref if n % pref == 0 else n


def _gate_act(z, idx):
    ls = _log_sigmoid(z)
    tg = I_CAP * jnp.tanh(z * (1.0 / I_CAP))
    return jnp.where(idx < 8, ls, jnp.where(idx < 12, tg, jnp.where(idx < 16, ls, 0.0)))


def _inproj_kernel(x_ref, wc_ref, bc_ref, wf_ref, bf_ref, wm_ref, bm_ref, ws_ref, bs_ref, wst_ref, bst_ref,
                   ufull_ref, ublk_ref, lblk_ref,
                   u_ref, fqb_ref, fk_ref, fv_ref, fkb_ref, fvb_ref, mq_ref, mk_ref, mv_ref, mo_ref,
                   cols_ref, ccols_ref, rows_ref, carry_ref):
    i = pl.program_id(0)

    @pl.when(i == 0)
    def _():
        carry_ref[...] = jnp.zeros_like(carry_ref)

    tm = x_ref.shape[0]
    xb = x_ref[...].astype(BF16)
    zc = _dot(xb, wc_ref[...]) + bc_ref[...]
    u_ref[...] = zc[:, :W_BR] * jax.nn.sigmoid(zc[:, W_BR:])

    zf = _dot(xb, wf_ref[...]) + bf_ref[...]
    fk = zf[:, W_BR:2 * W_BR]
    fv = zf[:, 2 * W_BR:]
    fqb_ref[...] = (zf[:, :W_BR] * DH_F ** -0.5).astype(BF16)
    fk_ref[...] = fk
    fv_ref[...] = fv
    fkb_ref[...] = fk.astype(BF16)
    fvb_ref[...] = fv.astype(BF16)

    zm = _dot(xb, wm_ref[...]) + bm_ref[...]
    mq_ref[...] = zm[:, :256]
    mk_ref[...] = zm[:, 256:512] * DK_M ** -0.5
    mv_ref[...] = zm[:, 512:1024]
    mo_ref[...] = jax.nn.sigmoid(zm[:, 1024:])

    zs = _dot(xb, ws_ref[...]) + bs_ref[...]
    lane = lax.broadcasted_iota(I32, zs.shape, 1)
    act_c = _gate_act(zs, lane)
    cols_ref[...] = act_c
    c1, c2, c3 = _split3(act_c)
    lb = lblk_ref[...]
    ccols_ref[...] = _dot(lb, c1) + _dot(lb, c2) + _dot(lb, c3)

    zt = _dot_nt(wst_ref[...], xb) + bst_ref[...]
    row = lax.broadcasted_iota(I32, zt.shape, 0)
    act_r = _gate_act(zt, row)
    r1, r2, r3 = _split3(act_r)
    uf = ufull_ref[...]
    ub = ublk_ref[...]
    cum_full = _dot(r1, uf) + _dot(r2, uf) + _dot(r3, uf) + carry_ref[:, 0:1]
    cum_blk = _dot(r1, ub) + _dot(r2, ub) + _dot(r3, ub)
    carry_ref[...] = jnp.broadcast_to(cum_full[:, tm - 1:tm], carry_ref.shape)
    rows_ref[0:16, :] = act_r
    rows_ref[16:32, :] = jnp.where(row < 8, cum_full, cum_blk)


def _tri_consts(tm, chunk):
    t = np.arange(tm)
    upper = (t[:, None] <= t[None, :])
    same = (t[:, None] // chunk) == (t[None, :] // chunk)
    ufull = jnp.asarray(upper, BF16)
    ublk = jnp.asarray(upper & same, BF16)
    lblk = jnp.asarray(upper.T & same, BF16)
    return ufull, ublk, lblk


def _split_w_in(w_in, b_in, b_fox_f, b_mlstm_f):
    o = np.cumsum([0, 512, 512, 512, 512, 512, 8, 256, 256, 512, 4, 4, 512, 3072])
    sl = lambda a, i, j: a[..., o[i]:o[j]]
    wc, bc = sl(w_in, 0, 2), sl(b_in, 0, 2)
    wf, bf = sl(w_in, 2, 5), sl(b_in, 2, 5)
    wm = jnp.concatenate([sl(w_in, 6, 9), sl(w_in, 11, 12)], axis=-1)
    bm = jnp.concatenate([sl(b_in, 6, 9), sl(b_in, 11, 12)], axis=-1)
    wsm = jnp.concatenate([sl(w_in, 5, 6), sl(w_in, 9, 11)], axis=-1)
    bsm = jnp.concatenate([sl(b_in, 5, 6) + b_fox_f, sl(b_in, 9, 10), sl(b_in, 10, 11) + b_mlstm_f], axis=-1)
    ws = jnp.pad(wsm, ((0, 0), (0, LANES - 16)))
    bs = jnp.pad(bsm, (0, LANES - 16))
    wg, bg = sl(w_in, 12, 13), sl(b_in, 12, 13)
    return dict(wc=wc.astype(BF16), bc=bc[None], wf=wf.astype(BF16), bf=bf[None], wm=wm.astype(BF16), bm=bm[None],
                ws=ws.astype(BF16), bs=bs[None], wst=wsm.T.astype(BF16), bst=bsm[:, None],
                wg=wg.astype(BF16), bg=bg[None])


def _inproj(x, w, tm):
    r = x.shape[0]
    chunk = min(M_CHUNK, tm)
    ufull, ublk, lblk = _tri_consts(tm, chunk)
    rowb = lambda n: pl.BlockSpec((tm, n), lambda i: (i, 0))
    full = lambda a: pl.BlockSpec(a.shape, lambda i: (0,) * a.ndim)
    consts = [w['wc'], w['bc'], w['wf'], w['bf'], w['wm'], w['bm'], w['ws'], w['bs'], w['wst'], w['bst'],
              ufull, ublk, lblk]
    out_shape = [jax.ShapeDtypeStruct((r, 512), F32), jax.ShapeDtypeStruct((r, 512), BF16),
                 jax.ShapeDtypeStruct((r, 512), F32), jax.ShapeDtypeStruct((r, 512), F32),
                 jax.ShapeDtypeStruct((r, 512), BF16), jax.ShapeDtypeStruct((r, 512), BF16),
                 jax.ShapeDtypeStruct((r, 256), F32), jax.ShapeDtypeStruct((r, 256), F32),
                 jax.ShapeDtypeStruct((r, 512), F32), jax.ShapeDtypeStruct((r, 512), F32),
                 jax.ShapeDtypeStruct((r, LANES), F32), jax.ShapeDtypeStruct((r, LANES), F32),
                 jax.ShapeDtypeStruct((32, r), F32)]
    out_specs = [rowb(512)] * 6 + [rowb(256)] * 2 + [rowb(512)] * 2 + [rowb(LANES)] * 2 + \
                [pl.BlockSpec((32, tm), lambda i: (0, i))]
    outs = pl.pallas_call(
        _inproj_kernel, grid=(r // tm,),
        in_specs=[rowb(D_MODEL)] + [full(a) for a in consts],
        out_specs=out_specs, out_shape=out_shape,
        scratch_shapes=[pltpu.VMEM((16, LANES), F32)],
        compiler_params=_cparams("arbitrary"), name="inproj",
    )(x, *consts)
    keys = ['u', 'fqb', 'fk', 'fv', 'fkb', 'fvb', 'mq', 'mk', 'mv', 'mo', 'cols', 'ccols', 'rows']
    return dict(zip(keys, outs))


HALO = 32


def _conv_kernel(u_ref, w_ref, b_ref, g_ref, bln_ref, y_ref, buf_ref):
    i = pl.program_id(0)
    tm = u_ref.shape[0]

    @pl.when(i == 0)
    def _():
        buf_ref[0:HALO, :] = jnp.zeros((HALO, W_BR), F32)

    buf_ref[HALO:HALO + tm, :] = u_ref[...]
    acc = jnp.zeros((tm, W_BR), F32) + b_ref[...]
    off = HALO - (K_CONV - 1)
    for k in range(K_CONV):
        acc = acc + w_ref[k:k + 1, :] * buf_ref[off + k:off + k + tm, :]
    yn = _layer_norm(acc, g_ref[...], bln_ref[...])
    y_ref[...] = (yn * jax.nn.sigmoid(yn)).astype(y_ref.dtype)
    buf_ref[0:HALO, :] = buf_ref[tm:tm + HALO, :]


def _conv_prompt(u, w_dw, b_dw, g, b, tm):
    s = u.shape[0]
    full = lambda a: pl.BlockSpec(a.shape, lambda i: (0,) * a.ndim)
    args = [w_dw, b_dw[None], g[None], b[None]]
    return pl.pallas_call(
        _conv_kernel, grid=(s // tm,),
        in_specs=[pl.BlockSpec((tm, W_BR), lambda i: (i, 0))] + [full(a) for a in args],
        out_specs=pl.BlockSpec((tm, W_BR), lambda i: (i, 0)),
        out_shape=jax.ShapeDtypeStruct((s, W_BR), BF16),
        scratch_shapes=[pltpu.VMEM((HALO + tm, W_BR), F32)],
        compiler_params=_cparams("arbitrary"), name="conv_prompt",
    )(u, *args)


def _conv_step_kernel(st_ref, u_ref, w_ref, b_ref, g_ref, bln_ref, y_ref, nst_ref):
    st = st_ref[0]
    un = u_ref[0]
    acc = b_ref[...] + w_ref[K_CONV - 1:K_CONV, :] * un
    acc = acc + jnp.sum(w_ref[0:K_CONV - 1, :] * st, axis=0, keepdims=True)
    yn = _layer_norm(acc, g_ref[...], bln_ref[...])
    y_ref[0] = (yn * jax.nn.sigmoid(yn)).astype(y_ref.dtype)
    nst_ref[0, 0:K_CONV - 2, :] = st[1:, :]
    nst_ref[0, K_CONV - 2:K_CONV - 1, :] = un


def _conv_step(state, u, w_dw, b_dw, g, b):
    bsz = u.shape[0]
    full = lambda a: pl.BlockSpec(a.shape, lambda i: (0,) * a.ndim)
    args = [w_dw, b_dw[None], g[None], b[None]]
    y, nst = pl.pallas_call(
        _conv_step_kernel, grid=(bsz,),
        in_specs=[pl.BlockSpec((1, K_CONV - 1, W_BR), lambda i: (i, 0, 0)),
                  pl.BlockSpec((1, 1, W_BR), lambda i: (i, 0, 0))] + [full(a) for a in args],
        out_specs=[pl.BlockSpec((1, 1, W_BR), lambda i: (i, 0, 0)),
                   pl.BlockSpec((1, K_CONV - 1, W_BR), lambda i: (i, 0, 0))],
        out_shape=[jax.ShapeDtypeStruct((bsz, 1, W_BR), BF16),
                   jax.ShapeDtypeStruct((bsz, K_CONV - 1, W_BR), F32)],
        compiler_params=_cparams("arbitrary"), name="conv_step",
    )(state, u[:, None, :], *args)
    return y[:, 0, :], nst


def _fox_prompt_kernel(q_ref, k_ref, v_ref, c_ref, o_ref, s_a, s_b, p_a, p_b, m_s, l_s, a_s, acc_s, *, tq, tk):
    qi = pl.program_id(1)
    q2 = q_ref[...].astype(F32)
    lane = lax.broadcasted_iota(I32, (tq, LANES), 1)
    q_start = pl.multiple_of(qi * tq, tq)
    n_full = (qi * tq) // tk
    qms = [jnp.where(lane < DH_F, q2, 0.0).astype(BF16), jnp.where(lane < DH_F, 0.0, q2).astype(BF16)]
    c0s = [c_ref[hh:hh + 1, pl.ds(q_start, LANES)][:, 0:1] for hh in range(2)]

    def scores(j, s_ref):
        k_start = pl.multiple_of(j * tk, tk)
        kb = k_ref[pl.ds(k_start, tk), :]
        for hh in range(2):
            s_ref[hh] = _dot_nt(qms[hh], kb) + (c0s[hh] - c_ref[hh:hh + 1, pl.ds(k_start, tk)])

    def values(j, p_ref):
        k_start = pl.multiple_of(j * tk, tk)
        vb = v_ref[pl.ds(k_start, tk), :]
        for hh in range(2):
            acc_s[hh] = a_s[hh] * acc_s[hh] + _dot(p_ref[hh], vb)

    def softmax(j, s_ref, p_ref, masked):
        for hh in range(2):
            s = s_ref[hh]
            if masked:
                rows = q_start + lax.broadcasted_iota(I32, (tq, tk), 0)
                cols = j * tk + lax.broadcasted_iota(I32, (tq, tk), 1)
                s = jnp.where(cols <= rows, s, NEG_BIG)
            m_old = m_s[hh]
            m_new = jnp.maximum(m_old, jnp.max(s, axis=1, keepdims=True))
            a = jnp.exp(m_old - m_new)
            p = jnp.exp(s - m_new)
            p_ref[hh] = p.astype(BF16)
            m_s[hh] = m_new
            l_s[hh] = a * l_s[hh] + jnp.sum(p, axis=1, keepdims=True)
            a_s[hh] = a

    def stage(j, s_cur, s_nxt, p_cur, p_prev, last):
        if not last:
            scores(j + 1, s_nxt)
        values(jnp.maximum(j - 1, 0), p_prev)
        softmax(j, s_cur, p_cur, last)
        if last:
            values(j, p_cur)

    def by_parity(j, last):
        even = lax.rem(j, 2) == 0
        pl.when(even)(functools.partial(stage, j, s_a, s_b, p_a, p_b, last))
        pl.when(jnp.logical_not(even))(functools.partial(stage, j, s_b, s_a, p_b, p_a, last))

    p_b[...] = jnp.zeros_like(p_b)
    m_s[...] = jnp.full_like(m_s, NEG_BIG)
    l_s[...] = jnp.zeros_like(l_s)
    a_s[...] = jnp.ones_like(a_s)
    acc_s[...] = jnp.zeros_like(acc_s)
    scores(0, s_a)

    def body(j, _):
        by_parity(j, False)
        return 0

    lax.fori_loop(0, n_full, body, 0)
    by_parity(n_full, True)
    o_ref[...] = jnp.where(lane < DH_F, acc_s[0] / l_s[0], acc_s[1] / l_s[1]).astype(o_ref.dtype)


def _fox_prompt(fqb, fkb, fvb, crow, tq, tk):
    s = fqb.shape[0]
    c3 = crow.reshape(H_F // 2, 2, s)
    return pl.pallas_call(
        functools.partial(_fox_prompt_kernel, tq=tq, tk=tk), grid=(H_F // 2, s // tq),
        in_specs=[pl.BlockSpec((tq, LANES), lambda p, i: (i, p)),
                  pl.BlockSpec((s, LANES), lambda p, i: (0, p)),
                  pl.BlockSpec((s, LANES), lambda p, i: (0, p)),
                  pl.BlockSpec((None, 2, s), lambda p, i: (p, 0, 0))],
        out_specs=pl.BlockSpec((tq, LANES), lambda p, i: (i, p)),
        out_shape=jax.ShapeDtypeStruct((s, W_BR), BF16),
        scratch_shapes=[pltpu.VMEM((2, tq, tk), F32), pltpu.VMEM((2, tq, tk), F32),
                        pltpu.VMEM((2, tq, tk), BF16), pltpu.VMEM((2, tq, tk), BF16),
                        pltpu.VMEM((2, tq, 1), F32), pltpu.VMEM((2, tq, 1), F32), pltpu.VMEM((2, tq, 1), F32),
                        pltpu.VMEM((2, tq, LANES), F32)],
        compiler_params=_cparams("arbitrary", "arbitrary"), name="fox_prompt",
    )(fqb, fkb, fvb, c3)


def _largest_divisor(n, options):
    return next(o for o in options if n % o == 0)


def _fox_suffix_kernel(pt_ref, *refs, g):
    lf_refs, lfn_ref, tri_ref, o_ref, carry_ref = refs[:g], refs[g], refs[g + 1], refs[g + 2], refs[g + 3]

    @pl.when(pl.program_id(1) == 0)
    def _():
        carry_ref[...] = jnp.zeros_like(carry_ref)

    tri = tri_ref[...]
    xs = [lf_refs[gg][...] for gg in range(g)]
    x1, x2, x3 = _split3(jnp.concatenate(xs, axis=0))
    within = _dot(x1, tri) + _dot(x2, tri) + _dot(x3, tri)
    carry = carry_ref[:, 0:1] + lfn_ref[...]
    for gg in reversed(range(g)):
        excl = within[gg * H_F:(gg + 1) * H_F] + carry
        o_ref[gg] = excl
        carry = excl[:, 0:1] + xs[gg][:, 0:1]
    carry_ref[...] = jnp.broadcast_to(carry - lfn_ref[...], carry_ref.shape)


def _fox_suffix(page_table, lf_t, lf_new_cols, layer, g):
    bsz, n_pages = page_table.shape
    page = lf_t.shape[-1]
    n_steps = n_pages // g
    t = np.arange(page)
    tri = jnp.asarray(t[:, None] > t[None, :], BF16)

    def page_spec(gg):
        return pl.BlockSpec((None, None, H_F, page),
                            lambda b, s, pt: (layer, pt[b, (n_steps - 1 - s) * g + gg], 0, 0))

    grid_spec = pltpu.PrefetchScalarGridSpec(
        num_scalar_prefetch=1, grid=(bsz, n_steps),
        in_specs=[page_spec(gg) for gg in range(g)] +
                 [pl.BlockSpec((None, H_F, 1), lambda b, s, pt: (b, 0, 0)),
                  pl.BlockSpec((page, page), lambda b, s, pt: (0, 0))],
        out_specs=pl.BlockSpec((None, g, H_F, page), lambda b, s, pt: (b, n_steps - 1 - s, 0, 0)),
        scratch_shapes=[pltpu.VMEM((H_F, LANES), F32)])
    return pl.pallas_call(
        functools.partial(_fox_suffix_kernel, g=g), grid_spec=grid_spec,
        out_shape=jax.ShapeDtypeStruct((bsz, n_pages, H_F, page), F32),
        compiler_params=_cparams("arbitrary", "arbitrary"), name="fox_suffix",
    )(page_table, *([lf_t] * g), lf_new_cols, tri)


def _fox_decode_kernel(pt_ref, *refs, g):
    q_ref, k_refs, v_refs = refs[0], refs[1:1 + g], refs[1 + g:1 + 2 * g]
    bias_ref, kn_ref, vn_ref, o_ref, qb_ref, m_ref, l_ref, acc_ref = refs[1 + 2 * g:]
    step = pl.program_id(1)
    page = k_refs[0].shape[-1]

    @pl.when(step == 0)
    def _():
        qb_ref[...] = jnp.broadcast_to(q_ref[...], qb_ref.shape)
        m_ref[...] = jnp.full_like(m_ref, NEG_BIG)
        l_ref[...] = jnp.zeros_like(l_ref)
        acc_ref[...] = jnp.zeros_like(acc_ref)

    for h in range(H_F):
        qb = qb_ref[h]
        rows = [jnp.sum(k_refs[gg][h] * qb, axis=0, keepdims=True) + bias_ref[gg, h:h + 1, :] for gg in range(g)]
        s = jnp.concatenate(rows, axis=0)
        m_old = m_ref[h][:, 0:1]
        m_new = jnp.maximum(m_old, jnp.max(jnp.max(s, axis=1, keepdims=True), axis=0, keepdims=True))
        a = jnp.exp(m_old - m_new)
        p = jnp.exp(s - m_new)
        l_new = a * l_ref[h][:, 0:1] + jnp.sum(jnp.sum(p, axis=1, keepdims=True), axis=0, keepdims=True)
        acc = a * acc_ref[h]
        for gg in range(g):
            acc = acc + v_refs[gg][h] * p[gg:gg + 1, :]
        m_ref[h] = jnp.broadcast_to(m_new, (1, page))
        l_ref[h] = jnp.broadcast_to(l_new, (1, page))
        acc_ref[h] = acc

    @pl.when(step == pl.num_programs(1) - 1)
    def _():
        for h in range(H_F):
            m_run = m_ref[h][:, 0:1]
            s_new = jnp.sum(q_ref[h] * kn_ref[h], axis=0, keepdims=True)
            m_fin = jnp.maximum(m_run, s_new)
            a2 = jnp.exp(m_run - m_fin)
            p_new = jnp.exp(s_new - m_fin)
            num = a2 * jnp.sum(acc_ref[h], axis=1, keepdims=True) + p_new * vn_ref[h]
            o_ref[h] = num / (a2 * l_ref[h][:, 0:1] + p_new)


def _fox_decode(page_table, fqb, fk_new, fv_new, lf_new, cache_k, cache_v, cache_lf, layer):
    bsz, n_pages = page_table.shape
    page = cache_k.shape[2]
    g = _largest_divisor(n_pages, (8, 4, 2, 1))
    n_steps = n_pages // g
    k_t = jnp.transpose(cache_k, (0, 1, 3, 4, 2))
    v_t = jnp.transpose(cache_v, (0, 1, 3, 4, 2))
    lf_t = jnp.transpose(cache_lf, (0, 1, 3, 2))
    bias = _fox_suffix(page_table, lf_t, lf_new[:, :, None], layer, _largest_divisor(n_pages, (16, 8, 4, 2, 1)))
    col = lambda a: a.astype(F32).reshape(bsz, H_F, DH_F, 1)

    def page_spec(gg):
        return pl.BlockSpec((None, None, H_F, DH_F, page), lambda b, s, pt: (layer, pt[b, s * g + gg], 0, 0, 0))

    per_b = pl.BlockSpec((None, H_F, DH_F, 1), lambda b, s, pt: (b, 0, 0, 0))
    grid_spec = pltpu.PrefetchScalarGridSpec(
        num_scalar_prefetch=1, grid=(bsz, n_steps),
        in_specs=[per_b] + [page_spec(gg) for gg in range(g)] * 2 +
                 [pl.BlockSpec((None, g, H_F, page), lambda b, s, pt: (b, s, 0, 0)), per_b, per_b],
        out_specs=per_b,
        scratch_shapes=[pltpu.VMEM((H_F, DH_F, page), F32), pltpu.VMEM((H_F, 1, page), F32),
                        pltpu.VMEM((H_F, 1, page), F32), pltpu.VMEM((H_F, DH_F, page), F32)])
    out = pl.pallas_call(
        functools.partial(_fox_decode_kernel, g=g), grid_spec=grid_spec,
        out_shape=jax.ShapeDtypeStruct((bsz, H_F, DH_F, 1), F32),
        compiler_params=_cparams("arbitrary", "arbitrary"), name="fox_decode",
    )(page_table, col(fqb), *([k_t] * g), *([v_t] * g), bias, col(fk_new), col(fv_new))
    return out.reshape(bsz, H_F * DH_F)


def _mlstm_prompt_kernel(q_ref, k_ref, v_ref, cols_ref, ccols_ref, rows_ref, h_ref, cst_ref, mst_ref,
                         c_scr, m_scr, *, n_chunks, ln):
    i = pl.program_id(0)

    @pl.when(i == 0)
    def _():
        c_scr[...] = jnp.zeros_like(c_scr)
        m_scr[...] = jnp.zeros_like(m_scr)

    lane = lax.broadcasted_iota(I32, (ln, LANES), 1)
    one_col = jnp.where(lane == 0, 1.0, 0.0).astype(F32)
    tril = lax.broadcasted_iota(I32, (ln, ln), 1) <= lax.broadcasted_iota(I32, (ln, ln), 0)
    srow = lax.broadcasted_iota(I32, (LANES, 2 * LANES), 0)

    def chunk(c, _):
        t0 = pl.multiple_of(c * ln, ln)
        cols = cols_ref[pl.ds(t0, ln), :]
        ccols = ccols_ref[pl.ds(t0, ln), :]
        rows = rows_ref[c]
        for p in range(H_M // 2):
            q2 = q_ref[pl.ds(t0, ln), p * LANES:(p + 1) * LANES]
            k2 = k_ref[pl.ds(t0, ln), p * LANES:(p + 1) * LANES].astype(BF16)
            c_pair = c_scr[p]
            c_pair_b = c_pair.astype(BF16)
            new_pair = c_pair
            for hh in range(2):
                h = 2 * p + hh
                own = (lane < DK_M) if hh == 0 else (lane >= DK_M)
                qm = jnp.where(own, q2, 0.0).astype(BF16)
                v = v_ref[pl.ds(t0, ln), h * DV_M:(h + 1) * DV_M]
                v_aug = jnp.concatenate([v, one_col], axis=1)
                ig_col = cols[:, 8 + h:9 + h]
                bc_col = ccols[:, 12 + h:13 + h]
                ig_row = rows[8 + h:9 + h, :]
                bc_row = rows[28 + h:29 + h, :]
                m0 = m_scr[h][0:1, 0:1]
                d = jnp.where(tril, bc_col - bc_row + ig_row, -jnp.inf)
                inter = bc_col + m0
                m_row = jnp.maximum(jnp.max(d, axis=1, keepdims=True), inter)
                s_in = jnp.exp(inter - m_row)
                sq = _dot_nt(qm, k2) * jnp.exp(d - m_row)
                num_aug = _dot(sq.astype(BF16), v_aug.astype(BF16)) + s_in * _dot(qm, c_pair_b)
                den = num_aug[:, DV_M:DV_M + 1]
                hv = num_aug[:, :DV_M] / jnp.maximum(jnp.abs(den), jnp.exp(-m_row))
                h_ref[pl.ds(t0, ln), h * DV_M:(h + 1) * DV_M] = hv
                b_last = bc_col[ln - 1:ln, :]
                dec = b_last - bc_col + ig_col
                m_new = jnp.maximum(b_last + m0, jnp.max(dec, axis=0, keepdims=True))
                ws = jnp.exp(dec - m_new)
                s0 = jnp.exp(b_last + m0 - m_new)
                upd = s0 * c_pair + _dot_tn(k2, (ws * v_aug).astype(BF16))
                rows_own = (srow < DK_M) if hh == 0 else (srow >= DK_M)
                new_pair = jnp.where(rows_own, upd, new_pair)
                m_scr[h] = jnp.broadcast_to(m_new, (8, LANES))
            c_scr[p] = new_pair
        return 0

    lax.fori_loop(0, n_chunks, chunk, 0)
    cst_ref[...] = c_scr[...]
    mst_ref[...] = m_scr[...]


def _mlstm_prompt(mq, mk, mv, cols, ccols, rows, g):
    s = mq.shape[0]
    ln = M_CHUNK
    tb = g * ln
    rows_ch = rows.reshape(32, s // ln, ln).transpose(1, 0, 2)
    rowb = lambda n: pl.BlockSpec((tb, n), lambda i: (i, 0))
    h, cst, mst = pl.pallas_call(
        functools.partial(_mlstm_prompt_kernel, n_chunks=g, ln=ln), grid=(s // tb,),
        in_specs=[rowb(256), rowb(256), rowb(512), rowb(LANES), rowb(LANES),
                  pl.BlockSpec((g, 32, ln), lambda i: (i, 0, 0))],
        out_specs=[rowb(512), pl.BlockSpec((2, LANES, 2 * LANES), lambda i: (0, 0, 0)),
                   pl.BlockSpec((H_M, 8, LANES), lambda i: (0, 0, 0))],
        out_shape=[jax.ShapeDtypeStruct((s, 512), F32), jax.ShapeDtypeStruct((2, LANES, 2 * LANES), F32),
                   jax.ShapeDtypeStruct((H_M, 8, LANES), F32)],
        scratch_shapes=[pltpu.VMEM((2, LANES, 2 * LANES), F32), pltpu.VMEM((H_M, 8, LANES), F32)],
        compiler_params=_cparams("arbitrary"), name="mlstm_prompt",
    )(mq, mk, mv, cols, ccols, rows_ch)
    c_new = cst[:, :, :DV_M].reshape(H_M, DK_M, DV_M)
    n_new = cst[:, :, DV_M].reshape(H_M, DK_M)
    m_new = mst[:, 0, 0]
    return h, c_new, n_new, m_new


def _mlstm_step_kernel(q_ref, k_ref, v_ref, cols_ref, c_ref, n_ref, m_ref, h_ref, cn_ref, nn_ref, mn_ref):
    cols = cols_ref[...]
    lane = lax.broadcasted_iota(I32, (1, LANES), 1)
    lane8 = lax.broadcasted_iota(I32, (8, LANES), 1)
    row8 = lax.broadcasted_iota(I32, (8, LANES), 0)
    srow = lax.broadcasted_iota(I32, (LANES, 1), 0)
    m_out = jnp.zeros((1, LANES), F32)
    for p in range(H_M // 2):
        q2 = q_ref[:, p * LANES:(p + 1) * LANES]
        k2 = k_ref[:, p * LANES:(p + 1) * LANES]
        q2r = q2.astype(BF16).astype(F32)
        k2r = k2.astype(BF16).astype(F32)
        n_pair = n_ref[:, p * LANES:(p + 1) * LANES]
        c_pair = c_ref[p * LANES:(p + 1) * LANES, :]
        c_pair_b = c_pair.astype(BF16)
        upd = jnp.zeros((LANES, LANES), F32)
        s0_rows = jnp.zeros((LANES, 1), F32)
        n_new = jnp.zeros((1, LANES), F32)
        for hh in range(2):
            h = 2 * p + hh
            own = (lane < DK_M) if hh == 0 else (lane >= DK_M)
            own8 = (lane8 < DK_M) if hh == 0 else (lane8 >= DK_M)
            qf = jnp.where(own, q2r, 0.0)
            v = v_ref[:, h * DV_M:(h + 1) * DV_M]
            ig = cols[:, 8 + h:9 + h]
            lf = cols[:, 12 + h:13 + h]
            m0 = m_ref[:, h:h + 1]
            inter = lf + m0
            m_row = jnp.maximum(ig, inter)
            s_in = jnp.exp(inter - m_row)
            wqk = jnp.sum(qf * k2r, axis=1, keepdims=True) * jnp.exp(ig - m_row)
            q_c = _dot(jnp.broadcast_to(qf, (8, LANES)).astype(BF16), c_pair_b)[0:1, :]
            q_n = jnp.sum(qf * n_pair.astype(BF16).astype(F32), axis=1, keepdims=True)
            num = wqk * v.astype(BF16).astype(F32) + s_in * q_c
            den = wqk + s_in * q_n
            h_ref[:, h * DV_M:(h + 1) * DV_M] = num / jnp.maximum(jnp.abs(den), jnp.exp(-m_row))
            ws = jnp.exp(ig - m_row)
            km8 = jnp.where(row8 == 0, jnp.where(own8, jnp.broadcast_to(k2r, (8, LANES)), 0.0), 0.0).astype(BF16)
            wv8 = jnp.where(row8 == 0, jnp.broadcast_to(ws * v, (8, LANES)), 0.0).astype(BF16)
            upd = upd + _dot_tn(km8, wv8)
            s0_rows = jnp.where((srow < DK_M) if hh == 0 else (srow >= DK_M), s_in, s0_rows)
            n_new = jnp.where(own, s_in * n_pair + ws * k2, n_new)
            m_out = jnp.where(lane == h, m_row, m_out)
        cn_ref[p * LANES:(p + 1) * LANES, :] = s0_rows * c_pair + upd
        nn_ref[:, p * LANES:(p + 1) * LANES] = n_new
    mn_ref[...] = m_out


def _mlstm_step(mq, mk, mv, cols, c0, n0, m0):
    bsz = mq.shape[0]
    b3 = lambda r, n: pl.BlockSpec((None, r, n), lambda i: (i, 0, 0))
    h, cn, nn, mn = pl.pallas_call(
        _mlstm_step_kernel, grid=(bsz,),
        in_specs=[b3(1, 256), b3(1, 256), b3(1, 512), b3(1, LANES), b3(H_M * DK_M, DV_M), b3(1, 256), b3(1, H_M)],
        out_specs=[b3(1, 512), b3(H_M * DK_M, DV_M), b3(1, 256), b3(1, LANES)],
        out_shape=[jax.ShapeDtypeStruct((bsz, 1, 512), F32), jax.ShapeDtypeStruct((bsz, H_M * DK_M, DV_M), F32),
                   jax.ShapeDtypeStruct((bsz, 1, 256), F32), jax.ShapeDtypeStruct((bsz, 1, LANES), F32)],
        compiler_params=_cparams("arbitrary"), name="mlstm_step",
    )(mq[:, None, :], mk[:, None, :], mv[:, None, :], cols[:, None, :],
      c0.reshape(bsz, H_M * DK_M, DV_M), n0.reshape(bsz, 1, H_M * DK_M), m0.reshape(bsz, 1, H_M))
    return (h[:, 0, :], cn.reshape(bsz, H_M, DK_M, DV_M), nn.reshape(bsz, H_M, DK_M), mn[:, 0, :H_M])


def _merge_kernel(x_ref, cy_ref, fy_ref, mh_ref, mo_ref, wg_ref, bg_ref, gm_ref, wb_ref, wo_ref, g_ref, b_ref, o_ref):
    x = x_ref[...]
    gates = jax.nn.sigmoid(_dot(x.astype(BF16), wg_ref[...]) + bg_ref[...])
    segs = []
    for h in range(H_M):
        seg = mh_ref[:, h * DV_M:(h + 1) * DV_M]
        segs.append(seg * lax.rsqrt(jnp.mean(seg * seg, axis=-1, keepdims=True) + LN_EPS))
    my = (jnp.concatenate(segs, axis=1) * gm_ref[...] * mo_ref[...]).astype(BF16)
    mix = gates[:, :D_MODEL] * _dot(cy_ref[...], wb_ref[0])
    mix = mix + gates[:, D_MODEL:2 * D_MODEL] * _dot(fy_ref[...], wb_ref[1])
    mix = mix + gates[:, 2 * D_MODEL:] * _dot(my, wb_ref[2])
    y = _dot(mix.astype(BF16), wo_ref[...])
    o_ref[...] = _layer_norm(ALPHA * x + y, g_ref[...], b_ref[...])


def _merge(x, conv_y, fox_y, m_h, m_o, w, tm):
    r = x.shape[0]
    rowb = lambda n: pl.BlockSpec((tm, n), lambda i: (i, 0))
    full = lambda a: pl.BlockSpec(a.shape, lambda i: (0,) * a.ndim)
    consts = [w['wg'], w['bg'], w['g_mnorm'], w['w_branch'], w['w_mix_out'], w['g_ln1'], w['b_ln1']]
    return pl.pallas_call(
        _merge_kernel, grid=(r // tm,),
        in_specs=[rowb(D_MODEL), rowb(512), rowb(512), rowb(512), rowb(512)] + [full(a) for a in consts],
        out_specs=rowb(D_MODEL), out_shape=jax.ShapeDtypeStruct((r, D_MODEL), F32),
        compiler_params=_cparams("arbitrary"), name="merge",
    )(x, conv_y, fox_y, m_h, m_o, *consts)


def _memkv_kernel(mem_ref, wk_ref, wv_ref, k_ref, v_ref, kb_ref, vb_ref):
    mb = mem_ref[...].astype(BF16)
    k = _dot(mb, wk_ref[...])
    v = _dot(mb, wv_ref[...])
    k_ref[...] = k
    v_ref[...] = v
    kb_ref[...] = k.astype(BF16)
    vb_ref[...] = v.astype(BF16)


def _memkv(mem, wk, wv):
    n = mem.shape[0]
    full = lambda a: pl.BlockSpec(a.shape, lambda i: (0,) * a.ndim)
    sd = lambda dt: jax.ShapeDtypeStruct((n, D_MODEL), dt)
    return pl.pallas_call(
        _memkv_kernel, grid=(1,),
        in_specs=[full(mem), full(wk), full(wv)],
        out_specs=[pl.BlockSpec((n, D_MODEL), lambda i: (0, 0))] * 4,
        out_shape=[sd(F32), sd(F32), sd(BF16), sd(BF16)],
        compiler_params=_cparams("arbitrary"), name="memkv",
    )(mem, wk, wv)


def _xattn_prompt_kernel(x_ref, kb_ref, vb_ref, wq_ref, wo_ref, g_ref, b_ref, o_ref):
    x = x_ref[...]
    q = (_dot(x.astype(BF16), wq_ref[...]) * DH_X ** -0.5).astype(BF16)
    outs = []
    for h in range(H_X):
        sl = slice(h * DH_X, (h + 1) * DH_X)
        s = _dot_nt(q[:, sl], kb_ref[:, sl])
        s = s - jnp.max(s, axis=1, keepdims=True)
        p = jnp.exp(s)
        o = _dot(p.astype(BF16), vb_ref[:, sl]) / jnp.sum(p, axis=1, keepdims=True)
        outs.append(o.astype(BF16))
    y = _dot(jnp.concatenate(outs, axis=1), wo_ref[...])
    o_ref[...] = _layer_norm(ALPHA * x + y, g_ref[...], b_ref[...])


def _xattn_prompt(x, kb, vb, w, tm):
    r = x.shape[0]
    rowb = pl.BlockSpec((tm, D_MODEL), lambda i: (i, 0))
    full = lambda a: pl.BlockSpec(a.shape, lambda i: (0,) * a.ndim)
    consts = [kb, vb, w['w_xq'], w['w_xo'], w['g_ln2'], w['b_ln2']]
    return pl.pallas_call(
        _xattn_prompt_kernel, grid=(r // tm,),
        in_specs=[rowb] + [full(a) for a in consts],
        out_specs=rowb, out_shape=jax.ShapeDtypeStruct((r, D_MODEL), F32),
        compiler_params=_cparams("arbitrary"), name="xattn_prompt",
    )(x, *consts)


def _linear_kernel(x_ref, w_ref, o_ref, *, scale):
    o_ref[...] = (_dot(x_ref[...].astype(BF16), w_ref[...]) * scale).astype(o_ref.dtype)


def _linear(x, w, scale, out_dtype):
    r, n = x.shape[0], w.shape[1]
    full = lambda a: pl.BlockSpec(a.shape, lambda i: (0,) * a.ndim)
    return pl.pallas_call(
        functools.partial(_linear_kernel, scale=scale), grid=(1,),
        in_specs=[full(x), full(w)], out_specs=pl.BlockSpec((r, n), lambda i: (0, 0)),
        out_shape=jax.ShapeDtypeStruct((r, n), out_dtype),
        compiler_params=_cparams("arbitrary"), name="linear",
    )(x, w)


def _linear_ln_kernel(x_ref, a_ref, w_ref, g_ref, b_ref, o_ref):
    y = _dot(a_ref[...].astype(BF16), w_ref[...])
    o_ref[...] = _layer_norm(ALPHA * x_ref[...] + y, g_ref[...], b_ref[...])


def _linear_ln(x, a, w, g, b):
    full = lambda t: pl.BlockSpec(t.shape, lambda i: (0,) * t.ndim)
    return pl.pallas_call(
        _linear_ln_kernel, grid=(1,),
        in_specs=[full(x), full(a), full(w), full(g), full(b)],
        out_specs=pl.BlockSpec(x.shape, lambda i: (0, 0)),
        out_shape=jax.ShapeDtypeStruct(x.shape, F32),
        compiler_params=_cparams("arbitrary"), name="linear_ln",
    )(x, a, w, g, b)


def _xattn_decode_kernel(q_ref, k_ref, v_ref, o_ref):
    w = q_ref.shape[-1]
    row = lax.broadcasted_iota(I32, (8, w), 0)
    lane = lax.broadcasted_iota(I32, (8, w), 1)
    own = (lane // DH_X) == row
    qrows = jnp.where(own, jnp.broadcast_to(q_ref[...].astype(F32), (8, w)), 0.0).astype(BF16)
    s = _dot_nt(qrows, k_ref[...].astype(BF16))
    s = s - jnp.max(s, axis=1, keepdims=True)
    p = jnp.exp(s)
    o = _dot(p.astype(BF16), v_ref[...].astype(BF16)) / jnp.sum(p, axis=1, keepdims=True)
    o_ref[...] = jnp.sum(jnp.where(own, o, 0.0), axis=0, keepdims=True)


def _xattn_decode(q, mem_k, mem_v):
    bsz, n_mem = mem_k.shape[0], mem_k.shape[1]
    b3 = lambda r: pl.BlockSpec((None, r, D_MODEL), lambda i: (i, 0, 0))
    out = pl.pallas_call(
        _xattn_decode_kernel, grid=(bsz,),
        in_specs=[b3(1), b3(n_mem), b3(n_mem)], out_specs=b3(1),
        out_shape=jax.ShapeDtypeStruct((bsz, 1, D_MODEL), F32),
        compiler_params=_cparams("arbitrary"), name="xattn_decode",
    )(q[:, None, :], mem_k.reshape(bsz, n_mem, D_MODEL), mem_v.reshape(bsz, n_mem, D_MODEL))
    return out[:, 0, :]


def _router_kernel(x_ref, w_ref, b_ref, lst_ref, ti_ref, tg_ref, cnt_ref, carry_ref):
    i = pl.program_id(0)

    @pl.when(i == 0)
    def _():
        carry_ref[...] = jnp.zeros_like(carry_ref)

    x1, x2, x3 = _split3(x_ref[...])
    w1, w2, w3 = w_ref[0], w_ref[1], w_ref[2]
    logits = (_dot(x1, w1) + (_dot(x1, w2) + _dot(x2, w1)) + (_dot(x1, w3) + _dot(x2, w2) + _dot(x3, w1))) + b_ref[...]
    tm = logits.shape[0]
    lane = lax.broadcasted_iota(I32, (tm, LANES), 1)
    lane_f = lane.astype(F32)
    vals = logits
    tops, idxs, sels = [], [], []
    for _ in range(TOP_K):
        mx = jnp.max(vals, axis=1, keepdims=True)
        idx = jnp.min(jnp.where(vals == mx, lane_f, float(LANES)), axis=1, keepdims=True)
        sel = lane_f == idx
        tops.append(mx)
        idxs.append(idx)
        sels.append(sel)
        vals = jnp.where(sel, -jnp.inf, vals)
    exps = [jnp.exp(t - tops[0]) for t in tops]
    den = exps[0] + exps[1] + exps[2] + exps[3]
    cnt = jnp.zeros((tm, LANES), F32)
    for sel in sels:
        cnt = cnt + jnp.where(sel, 1.0, 0.0)
    excl = _dot(lst_ref[...], cnt.astype(BF16)) + carry_ref[0:1, :]
    ti = jnp.zeros((tm, LANES), F32)
    tg = jnp.zeros((tm, LANES), F32)
    for k in range(TOP_K):
        rank = jnp.sum(jnp.where(sels[k], excl, 0.0), axis=1, keepdims=True)
        ti = jnp.where(lane == k, idxs[k], ti)
        ti = jnp.where(lane == TOP_K + k, rank, ti)
        tg = jnp.where(lane == k, exps[k] / den, tg)
    ti_ref[...] = ti.astype(I32)
    tg_ref[...] = tg
    carry_ref[...] = carry_ref[...] + jnp.sum(cnt, axis=0, keepdims=True)
    cnt_ref[...] = carry_ref[...]


def _router(x, w3, b, tm):
    r = x.shape[0]
    t = np.arange(tm)
    lst = jnp.asarray(t[:, None] > t[None, :], BF16)
    full = lambda a: pl.BlockSpec(a.shape, lambda i: (0,) * a.ndim)
    rowb = lambda n: pl.BlockSpec((tm, n), lambda i: (i, 0))
    return pl.pallas_call(
        _router_kernel, grid=(r // tm,),
        in_specs=[rowb(D_MODEL), full(w3), full(b), full(lst)],
        out_specs=[rowb(LANES), rowb(LANES), pl.BlockSpec((8, LANES), lambda i: (0, 0))],
        out_shape=[jax.ShapeDtypeStruct((r, LANES), I32), jax.ShapeDtypeStruct((r, LANES), F32),
                   jax.ShapeDtypeStruct((8, LANES), F32)],
        scratch_shapes=[pltpu.VMEM((8, LANES), F32)],
        compiler_params=_cparams("arbitrary"), name="router",
    )(x, w3, b, lst)


def _moe_dispatch_kernel(idx_hbm, x_ref, xs_in, xs_hbm, idx_smem, sem_idx, sem):
    del xs_in
    i = pl.program_id(0)
    tm = x_ref.shape[0]
    idx_cp = pltpu.make_async_copy(idx_hbm.at[i], idx_smem, sem_idx)
    idx_cp.start()
    idx_cp.wait()

    def send(r, _):
        for k in range(TOP_K):
            pltpu.make_async_copy(x_ref.at[pl.ds(r, 1)], xs_hbm.at[pl.ds(idx_smem[r * TOP_K + k], 1)], sem).start()
        return 0

    lax.fori_loop(0, tm, send, 0)
    for k in range(TOP_K):
        pltpu.make_async_copy(x_ref, xs_hbm.at[pl.ds(0, tm)], sem).wait()


def _moe_dispatch(idx, x, rows, tm):
    n = x.shape[0]
    return pl.pallas_call(
        _moe_dispatch_kernel, grid=(n // tm,),
        in_specs=[pl.BlockSpec(memory_space=pl.ANY), pl.BlockSpec((tm, D_MODEL), lambda i: (i, 0)),
                  pl.BlockSpec(memory_space=pl.ANY)],
        out_specs=pl.BlockSpec(memory_space=pl.ANY),
        out_shape=jax.ShapeDtypeStruct((rows, D_MODEL), F32),
        scratch_shapes=[pltpu.SMEM((tm * TOP_K,), I32), pltpu.SemaphoreType.DMA(()), pltpu.SemaphoreType.DMA(())],
        input_output_aliases={2: 0},
        compiler_params=_cparams("arbitrary"), name="moe_dispatch",
    )(idx, x, jnp.zeros((rows, D_MODEL), F32))


def _moe_ffn_kernel(be_ref, nv_ref, xs_ref, wgu_ref, bgu_ref, wd_ref, bd_ref, y_ref, wgu_b, wd_b):
    i = pl.program_id(0)

    @pl.when(i < nv_ref[0])
    def _():
        e = be_ref[i]
        e_prev = be_ref[jnp.maximum(i - 1, 0)]

        @pl.when((i == 0) | (e != e_prev))
        def _():
            wgu_b[...] = wgu_ref[...].astype(BF16)
            wd_b[...] = wd_ref[...].astype(BF16)

        gu = _dot(xs_ref[...].astype(BF16), wgu_b[...]) + bgu_ref[...]
        g = jnp.minimum(gu[:, :D_EXPERT], SWIGLU_LIMIT)
        u = jnp.clip(gu[:, D_EXPERT:], -SWIGLU_LIMIT, SWIGLU_LIMIT)
        act = (u + 1.0) * g * jax.nn.sigmoid(SWIGLU_ALPHA * g)
        y_ref[...] = _dot(act.astype(BF16), wd_b[...]) + bd_ref[...]

    @pl.when(i >= nv_ref[0])
    def _():
        y_ref[...] = jnp.zeros_like(y_ref)


def _moe_ffn(blk_e, n_valid, xs, w, l, tb):
    rows = xs.shape[0]
    wsel = lambda i, be, nv: (l, be[i], 0, 0)
    grid_spec = pltpu.PrefetchScalarGridSpec(
        num_scalar_prefetch=2, grid=(rows // tb,),
        in_specs=[pl.BlockSpec((tb, D_MODEL), lambda i, be, nv: (jnp.minimum(i, nv[0] - 1), 0)),
                  pl.BlockSpec((None, None, D_MODEL, 2 * D_EXPERT), wsel),
                  pl.BlockSpec((None, None, 1, 2 * D_EXPERT), wsel),
                  pl.BlockSpec((None, None, D_EXPERT, D_MODEL), wsel),
                  pl.BlockSpec((None, None, 1, D_MODEL), wsel)],
        out_specs=pl.BlockSpec((tb, D_MODEL), lambda i, be, nv: (i, 0)),
        scratch_shapes=[pltpu.VMEM((D_MODEL, 2 * D_EXPERT), BF16), pltpu.VMEM((D_EXPERT, D_MODEL), BF16)])
    return pl.pallas_call(
        _moe_ffn_kernel, grid_spec=grid_spec,
        out_shape=jax.ShapeDtypeStruct((rows, D_MODEL), F32),
        compiler_params=_cparams("arbitrary"), name="moe_ffn",
    )(blk_e, n_valid, xs, w['w_gate_up'], w['b_gate_up'][:, :, None, :], w['w_down'], w['b_down'][:, :, None, :])


def _moe_combine_kernel(idx_hbm, x_ref, tg_ref, ys_hbm, g_ref, b_ref, o_ref, idx_smem, ybuf, sem_idx, sem):
    i = pl.program_id(0)
    tm = x_ref.shape[0]
    idx_cp = pltpu.make_async_copy(idx_hbm.at[i], idx_smem, sem_idx)
    idx_cp.start()
    idx_cp.wait()

    def fetch(r, _):
        for k in range(TOP_K):
            pltpu.make_async_copy(ys_hbm.at[pl.ds(idx_smem[r * TOP_K + k], 1)], ybuf.at[k, pl.ds(r, 1)], sem).start()
        return 0

    lax.fori_loop(0, tm, fetch, 0)
    for k in range(TOP_K):
        pltpu.make_async_copy(ys_hbm.at[pl.ds(0, tm)], ybuf.at[k], sem).wait()
    tg = tg_ref[...]
    ff = tg[:, 0:1] * ybuf[0]
    for k in range(1, TOP_K):
        ff = ff + tg[:, k:k + 1] * ybuf[k]
    o_ref[...] = _layer_norm(ALPHA * x_ref[...] + ff, g_ref[...], b_ref[...])


def _moe_combine(idx, x, tg, ys, g, b, tm):
    n = x.shape[0]
    rowb = lambda wd: pl.BlockSpec((tm, wd), lambda i: (i, 0))
    full = lambda t: pl.BlockSpec(t.shape, lambda i: (0,) * t.ndim)
    return pl.pallas_call(
        _moe_combine_kernel, grid=(n // tm,),
        in_specs=[pl.BlockSpec(memory_space=pl.ANY), rowb(D_MODEL), rowb(LANES), pl.BlockSpec(memory_space=pl.ANY),
                  full(g), full(b)],
        out_specs=rowb(D_MODEL), out_shape=jax.ShapeDtypeStruct((n, D_MODEL), F32),
        scratch_shapes=[pltpu.SMEM((tm * TOP_K,), I32), pltpu.VMEM((TOP_K, tm, D_MODEL), F32),
                        pltpu.SemaphoreType.DMA(()), pltpu.SemaphoreType.DMA(())],
        compiler_params=_cparams("arbitrary"), name="moe_combine",
    )(idx, x, tg, ys, g, b)


def _moe(x, w, l, tb, tm):
    n = x.shape[0]
    a = n * TOP_K
    ti, tg, cnt = _router(x, w['w_router3'], w['b_router'], tm)
    top_e = ti[:, :TOP_K]
    rank = ti[:, TOP_K:2 * TOP_K]
    counts = cnt[0, :N_EXPERTS].astype(I32)
    n_blk = -(-a // tb) + N_EXPERTS
    padded = (counts + tb - 1) // tb * tb
    pend = jnp.cumsum(padded)
    dest = (pend - padded)[top_e] + rank
    idx = dest.reshape(n // tm, tm * TOP_K)
    n_valid = (pend[-1:] // tb).astype(I32)
    blk_id = jnp.minimum(jnp.arange(n_blk, dtype=I32), n_valid - 1)
    blk_e = jnp.minimum(jnp.sum((pend[None, :] <= (blk_id * tb)[:, None]).astype(I32), axis=1), N_EXPERTS - 1)
    xs = _moe_dispatch(idx, x, n_blk * tb, tm)
    ys = _moe_ffn(blk_e, n_valid, xs, w, l, tb)
    return _moe_combine(idx, x, tg, ys, w['g_ln3'], w['b_ln3'], tm)


def _layer_weights(l, w_in, b_in, b_fox_f, b_mlstm_f, g_mlstm_norm, w_branch, w_mix_out, g_ln1, b_ln1,
                   w_xq, w_xk, w_xv, w_xo, g_ln2, b_ln2, w_router, b_router, w_gate_up, b_gate_up,
                   w_down, b_down, g_ln3, b_ln3):
    w = _split_w_in(w_in[l], b_in[l], b_fox_f[l], b_mlstm_f[l])
    wr = jnp.pad(w_router[l], ((0, 0), (0, LANES - N_EXPERTS)))
    r1 = wr.astype(BF16)
    r2 = (wr - r1.astype(F32)).astype(BF16)
    r3 = (wr - r1.astype(F32) - r2.astype(F32)).astype(BF16)
    w.update(
        g_mnorm=g_mlstm_norm[l][None], w_branch=w_branch[l].astype(BF16), w_mix_out=w_mix_out[l].astype(BF16),
        g_ln1=g_ln1[l][None], b_ln1=b_ln1[l][None],
        w_xq=w_xq[l].astype(BF16), w_xk=w_xk[l].astype(BF16), w_xv=w_xv[l].astype(BF16), w_xo=w_xo[l].astype(BF16),
        g_ln2=g_ln2[l][None], b_ln2=b_ln2[l][None],
        w_router3=jnp.stack([r1, r2, r3]),
        b_router=jnp.pad(b_router[l], (0, LANES - N_EXPERTS), constant_values=NEG_BIG)[None],
        w_gate_up=w_gate_up, b_gate_up=b_gate_up, w_down=w_down, b_down=b_down,
        g_ln3=g_ln3[l][None], b_ln3=b_ln3[l][None])
    return w


def kernel(x_prompt, x_sample, mem_prompt, cache_fox_k, cache_fox_v, cache_fox_lf, page_table, state_conv,
           state_mlstm_c, state_mlstm_n, state_mlstm_m, cache_mem_k, cache_mem_v, w_in, b_in, b_fox_f, b_mlstm_f,
           w_dw, b_dw, g_conv_ln, b_conv_ln, g_mlstm_norm, w_branch, w_mix_out, g_ln1, b_ln1, w_xq, w_xk, w_xv,
           w_xo, g_ln2, b_ln2, w_router, b_router, w_gate_up, b_gate_up, w_down, b_down, g_ln3, b_ln3):
    b_p, s, _ = x_prompt.shape
    b_s = x_sample.shape[0]
    assert b_p == 1 and x_sample.shape[1] == 1
    depth = w_in.shape[0]
    xp = x_prompt.reshape(s, D_MODEL)
    xs = x_sample.reshape(b_s, D_MODEL)
    mem = mem_prompt.reshape(-1, D_MODEL)
    tm = _row_block(s, 256)
    tq = _row_block(s, 512)
    tk = _row_block(s, 1024)
    g_chunks = 8 if s % (8 * M_CHUNK) == 0 else 1
    tb_p = 256 if s * TOP_K >= 256 * N_EXPERTS else 8
    rows_p, rows_s = [], []
    for l in range(depth):
        w = _layer_weights(l, w_in, b_in, b_fox_f, b_mlstm_f, g_mlstm_norm, w_branch, w_mix_out, g_ln1, b_ln1,
                           w_xq, w_xk, w_xv, w_xo, g_ln2, b_ln2, w_router, b_router, w_gate_up, b_gate_up,
                           w_down, b_down, g_ln3, b_ln3)
        pr = _inproj(xp, w, tm)
        conv_y = _conv_prompt(pr['u'], w_dw[l], b_dw[l], g_conv_ln[l], b_conv_ln[l], tm)
        fox_y = _fox_prompt(pr['fqb'], pr['fkb'], pr['fvb'], pr['rows'][16:24], tq, tk)
        m_h, m_c, m_n, m_m = _mlstm_prompt(pr['mq'], pr['mk'], pr['mv'], pr['cols'], pr['ccols'], pr['rows'], g_chunks)
        xp = _merge(xp, conv_y, fox_y, m_h, pr['mo'], w, tm)
        mem_k, mem_v, mem_kb, mem_vb = _memkv(mem, w['w_xk'], w['w_xv'])
        xp = _xattn_prompt(xp, mem_kb, mem_vb, w, tm)
        xp = _moe(xp, w, l, tb_p, tm)
        rows_p.append((pr['fk'].reshape(1, s, H_F, DH_F), pr['fv'].reshape(1, s, H_F, DH_F),
                       pr['cols'][:, :H_F].reshape(1, s, H_F), pr['u'][s - (K_CONV - 1):][None],
                       m_c[None], m_n[None], m_m[None],
                       mem_k.reshape(1, -1, H_X, DH_X), mem_v.reshape(1, -1, H_X, DH_X)))
        sr = _inproj(xs, w, b_s)
        conv_y, conv_st = _conv_step(state_conv[l], sr['u'], w_dw[l], b_dw[l], g_conv_ln[l], b_conv_ln[l])
        lf_new = sr['cols'][:, :H_F]
        fox_y = _fox_decode(page_table, sr['fqb'], sr['fk'], sr['fv'], lf_new, cache_fox_k, cache_fox_v,
                            cache_fox_lf, l)
        m_h, m_c, m_n, m_m = _mlstm_step(sr['mq'], sr['mk'], sr['mv'], sr['cols'], state_mlstm_c[l],
                                         state_mlstm_n[l], state_mlstm_m[l])
        xs = _merge(xs, conv_y, fox_y.astype(BF16), m_h, sr['mo'], w, b_s)
        q = _linear(xs, w['w_xq'], DH_X ** -0.5, BF16)
        att = _xattn_decode(q, cache_mem_k[l], cache_mem_v[l])
        xs = _linear_ln(xs, att, w['w_xo'], w['g_ln2'], w['b_ln2'])
        xs = _moe(xs, w, l, 8, b_s)
        rows_s.append((sr['fk'].reshape(b_s, 1, H_F, DH_F), sr['fv'].reshape(b_s, 1, H_F, DH_F),
                       lf_new.reshape(b_s, 1, H_F), conv_st, m_c, m_n, m_m))
    outs_p = [jnp.stack(a) for a in zip(*rows_p)]
    outs_s = [jnp.stack(a) for a in zip(*rows_s)]
    return (xp.reshape(1, s, D_MODEL), xs.reshape(b_s, 1, D_MODEL), *outs_p, *outs_s)
```
